```python
import math
import jax, jax.numpy as jnp
from jax import lax
import numpy as np

D_MODEL = 1024
BATCH = 2
SEQ = 16384
DEPTH = 2
DEC_BATCH = 8
DEC_SEQ = 8192
PAST_LEN = 128

N_EVEN = (DEPTH + 1) // 2
N_ODD = DEPTH // 2
EPS = 1e-6
CHUNK = 128
Q_BLOCK = 128

ML_H = 4
ML_DH = 128
ML_W = ML_H * ML_DH
ML_FGATE_BIAS_LO = 3.0
ML_FGATE_BIAS_HI = 6.0
MLA_H = 4
MLA_NOPE = 128
MLA_ROPE = 64
MLA_V = 128
MLA_Q_LORA = 256
MLA_KV_LORA = 128
MLA_THETA = 10000.0
RET_H = 4
RET_DK = 128
RET_DV = 128
RET_THETA = 10000.0
RET_DECAY_EXP_FWD = 5.0
RET_DECAY_EXP_BWD = 5.5
DF_H = 4
DF_DH = 64
DF_DV = 2 * DF_DH
ROPE_THETA = 500000.0
ROPE_DIMS = DF_DH // 4
N_MEM = 256
XA_H = 4
XA_DH = 128
MOE_GROUPS = 4
MOE_PER_GROUP = 8
MOE_EXPERTS = MOE_GROUPS * MOE_PER_GROUP
MOE_TOPK = 2
MOE_FF = 512
MOE_BLOCK = 128

EV_IN = 4 * ML_W + 4 * ML_H + MLA_Q_LORA + MLA_KV_LORA + MLA_ROPE
EV_OUT = ML_W + MLA_H * MLA_V
OD_IN = 2 * RET_H * RET_DK + 2 * RET_H * RET_DV + 2 * DF_H * 2 * DF_DH + DF_H * DF_DV
OD_OUT = RET_H * RET_DV + DF_H * DF_DV

kernel_name = 'hybrid_bidir_mlstm_mla_retnet_diffattn_hmoe'

F32 = jnp.float32


def rmsnorm(x, g):
    xf = x.astype(F32)
    y = xf * lax.rsqrt(jnp.mean(xf * xf, axis=-1, keepdims=True) + EPS)
    return (y * g.astype(F32)).astype(x.dtype)


def split_cols(z, sizes):
    outs, off = [], 0
    for s in sizes:
        outs.append(z[..., off:off + s])
        off += s
    return outs


def rope_tables(seq, dim, theta):
    inv = 1.0 / (theta ** (jnp.arange(0, dim, 2, dtype=F32) / dim))
    ang = jnp.arange(seq, dtype=F32)[:, None] * inv[None, :]
    return jnp.cos(ang), jnp.sin(ang)


def apply_rope(x, cos, sin):
    x1, x2 = jnp.split(x, 2, axis=-1)
    c = cos[:, None, :].astype(x.dtype)
    s = sin[:, None, :].astype(x.dtype)
    return jnp.concatenate([x1 * c - x2 * s, x1 * s + x2 * c], axis=-1)


def to_chunks(t):
    b, s = t.shape[0], t.shape[1]
    t = t.reshape(b, s // CHUNK, CHUNK, *t.shape[2:])
    return jnp.moveaxis(t, (1, 3), (0, 2))


def from_chunks(t):
    nc, b, h, l, d = t.shape
    return jnp.moveaxis(t, (0, 2), (1, 3)).reshape(b, nc * l, h, d)


def mlstm_chunkwise(q, k, v, log_i, log_f):
    B, S, H, d = q.shape
    causal = jnp.tril(jnp.ones((CHUNK, CHUNK), dtype=bool))

    def step(carry, xs):
        C, n, m = carry
        qc, kc, vc, ic, fc = xs
        b = jnp.cumsum(fc, axis=-1)
        logD = jnp.where(causal, b[..., :, None] - b[..., None, :] + ic[..., None, :], -jnp.inf)
        inter = b + m[..., None]
        m_t = jnp.maximum(inter, jnp.max(logD, axis=-1))
        a_inter = jnp.exp(inter - m_t)
        s = jnp.einsum('bhtd,bhsd->bhts', qc, kc) * jnp.exp(logD - m_t[..., None])
        num = jnp.einsum('bhts,bhsd->bhtd', s, vc) + a_inter[..., None] * jnp.einsum('bhtd,bhde->bhte', qc, C)
        den = jnp.sum(s, axis=-1) + a_inter * jnp.einsum('bhtd,bhd->bht', qc, n)
        h = num / jnp.maximum(jnp.abs(den), jnp.exp(-m_t))[..., None]
        bL = b[..., -1]
        w_log = bL[..., None] - b + ic
        m_new = jnp.maximum(bL + m, jnp.max(w_log, axis=-1))
        dec = jnp.exp(bL + m - m_new)
        ws = jnp.exp(w_log - m_new[..., None])
        C_new = dec[..., None, None] * C + jnp.einsum('bhs,bhsd,bhse->bhde', ws, kc, vc)
        n_new = dec[..., None] * n + jnp.einsum('bhs,bhsd->bhd', ws, kc)
        return (C_new, n_new, m_new), h

    init = (jnp.zeros((B, H, d, d), F32), jnp.zeros((B, H, d), F32), jnp.zeros((B, H), F32))
    _, hs = lax.scan(step, init, (to_chunks(q), to_chunks(k), to_chunks(v), to_chunks(log_i), to_chunks(log_f)))
    return from_chunks(hs)


def retention_cross(qc, kc, vc, log_g):
    _, B, H, L, dk = qc.shape
    dv = vc.shape[-1]
    pos = jnp.arange(L, dtype=F32)
    q_dec = jnp.exp((pos + 1.0)[None, :] * log_g[:, None])
    k_dec = jnp.exp((L - 1.0 - pos)[None, :] * log_g[:, None])
    chunk_dec = jnp.exp(L * log_g)

    def step(state, xs):
        qx, kx, vx = xs
        out = jnp.einsum('bhld,bhde->bhle', qx, state) * q_dec[None, :, :, None]
        state = chunk_dec[None, :, None, None] * state + jnp.einsum('bhld,bhle->bhde', kx * k_dec[None, :, :, None], vx)
        return state, out

    _, outs = lax.scan(step, jnp.zeros((B, H, dk, dv), F32), (qc, kc, vc))
    return outs


def retention_bidir(q, k, v):
    H = q.shape[2]
    heads = jnp.arange(H, dtype=F32)
    lg_f = jnp.log1p(-jnp.exp2(-RET_DECAY_EXP_FWD - heads))
    lg_b = jnp.log1p(-jnp.exp2(-RET_DECAY_EXP_BWD - heads))
    qc, kc, vc = to_chunks(q), to_chunks(k), to_chunks(v)
    pos = jnp.arange(CHUNK, dtype=F32)
    dist = pos[:, None] - pos[None, :]
    adist = jnp.abs(dist)[None]
    M = jnp.where(dist[None] >= 0, jnp.exp(adist * lg_f[:, None, None]), jnp.exp(adist * lg_b[:, None, None]))
    scores = jnp.einsum('nbhld,nbhmd->nbhlm', qc, kc) * M[None, None]
    y = jnp.einsum('nbhlm,nbhme->nbhle', scores, vc)
    fl = lambda t: jnp.flip(t, axis=(0, 3))
    y = y + retention_cross(qc, kc, vc, lg_f) + fl(retention_cross(fl(qc), fl(kc), fl(vc), lg_b))
    return from_chunks(y)


def blocked_softmax_attention(q, k, v, scale):
    B, S, H, dq = q.shape
    nb = S // Q_BLOCK
    qb = jnp.moveaxis(q.reshape(B, nb, Q_BLOCK, H, dq), 1, 0)

    def one(qblk):
        s = jnp.einsum('bqhd,bkhd->bhqk', qblk, k).astype(F32) * scale
        p = jax.nn.softmax(s, axis=-1).astype(v.dtype)
        return jnp.einsum('bhqk,bkhd->bqhd', p, v)

    out = lax.map(one, qb)
    return jnp.moveaxis(out, 0, 1).reshape(B, S, H, v.shape[-1])


def blocked_diff_attention(q1, q2, k1, k2, v, lam, scale):
    B, S, H, d = q1.shape
    nb = S // Q_BLOCK
    blk = lambda t: jnp.moveaxis(t.reshape(B, nb, Q_BLOCK, H, d), 1, 0)

    def one(args):
        a, b = args
        p1 = jax.nn.softmax(jnp.einsum('bqhd,bkhd->bhqk', a, k1).astype(F32) * scale, axis=-1)
        p2 = jax.nn.softmax(jnp.einsum('bqhd,bkhd->bhqk', b, k2).astype(F32) * scale, axis=-1)
        p = (p1 - lam * p2).astype(v.dtype)
        return jnp.einsum('bhqk,bkhd->bqhd', p, v)

    out = lax.map(one, (blk(q1), blk(q2)))
    return jnp.moveaxis(out, 0, 1).reshape(B, S, H, v.shape[-1])


def even_mixer(h, w_in, gate_bias, w_out, ml_hn, qa_n, kva_n, w_uq, w_ukv, q_n, k_n):
    B, S, _ = h.shape
    z = h @ w_in
    mq, mk, mv, mo, mg, cq, ckv, kpe = split_cols(z, [ML_W, ML_W, ML_W, ML_W, 4 * ML_H, MLA_Q_LORA, MLA_KV_LORA, MLA_ROPE])
    q = mq.reshape(B, S, ML_H, ML_DH).astype(F32)
    k = mk.reshape(B, S, ML_H, ML_DH).astype(F32) * (ML_DH ** -0.5)
    v = mv.reshape(B, S, ML_H, ML_DH).astype(F32)
    g = mg.reshape(B, S, 4, ML_H).astype(F32) + gate_bias.astype(F32)
    i_f, lf_f = g[:, :, 0], jax.nn.log_sigmoid(g[:, :, 1])
    i_b, lf_b = g[:, :, 2], jax.nn.log_sigmoid(g[:, :, 3])
    fl = lambda t: jnp.flip(t, axis=1)
    h_ml = mlstm_chunkwise(q, k, v, i_f, lf_f) + fl(mlstm_chunkwise(fl(q), fl(k), fl(v), fl(i_b), fl(lf_b)))
    h_ml = rmsnorm(h_ml, ml_hn.reshape(ML_H, ML_DH)).reshape(B, S, ML_W)
    h_ml = (h_ml * jax.nn.sigmoid(mo.astype(F32))).astype(h.dtype)
    qf = (rmsnorm(cq, qa_n) @ w_uq).reshape(B, S, MLA_H, MLA_NOPE + MLA_ROPE)
    kv = (rmsnorm(ckv, kva_n) @ w_ukv).reshape(B, S, MLA_H, MLA_NOPE + MLA_V)
    k_nope, v_mla = kv[..., :MLA_NOPE], kv[..., MLA_NOPE:]
    k_pe = jnp.broadcast_to(kpe[:, :, None, :], (B, S, MLA_H, MLA_ROPE))
    kf = jnp.concatenate([k_nope, k_pe], axis=-1)
    qf = rmsnorm(qf, q_n)
    kf = rmsnorm(kf, k_n)
    cos, sin = rope_tables(S, MLA_ROPE, MLA_THETA)
    qf = jnp.concatenate([qf[..., :MLA_NOPE], apply_rope(qf[..., MLA_NOPE:], cos, sin)], axis=-1)
    kf = jnp.concatenate([kf[..., :MLA_NOPE], apply_rope(kf[..., MLA_NOPE:], cos, sin)], axis=-1)
    o_mla = blocked_softmax_attention(qf, kf, v_mla, (MLA_NOPE + MLA_ROPE) ** -0.5).reshape(B, S, MLA_H * MLA_V)
    return jnp.concatenate([h_ml, o_mla.astype(h.dtype)], axis=-1) @ w_out


def odd_mixer(h, layer_idx, w_in, w_out, ret_hn, lam_p, q_n, k_n, subln):
    B, S, _ = h.shape
    z = h @ w_in
    rq, rk, rv, rg, dq, dk, dv = split_cols(z, [RET_H * RET_DK, RET_H * RET_DK, RET_H * RET_DV, RET_H * RET_DV,
                                                DF_H * 2 * DF_DH, DF_H * 2 * DF_DH, DF_H * DF_DV])
    cos, sin = rope_tables(S, RET_DK, RET_THETA)
    q = apply_rope(rq.reshape(B, S, RET_H, RET_DK), cos, sin).astype(F32)
    k = apply_rope(rk.reshape(B, S, RET_H, RET_DK), cos, sin).astype(F32) * (RET_DK ** -0.5)
    v = rv.reshape(B, S, RET_H, RET_DV).astype(F32)
    y_ret = rmsnorm(retention_bidir(q, k, v), ret_hn.reshape(RET_H, RET_DV)).reshape(B, S, RET_H * RET_DV)
    y_ret = (y_ret * jax.nn.silu(rg.astype(F32))).astype(h.dtype)
    qd = rmsnorm(dq.reshape(B, S, DF_H * 2, DF_DH), q_n)
    kd = rmsnorm(dk.reshape(B, S, DF_H * 2, DF_DH), k_n)
    cos, sin = rope_tables(S, ROPE_DIMS, ROPE_THETA)
    qd = jnp.concatenate([apply_rope(qd[..., :ROPE_DIMS], cos, sin), qd[..., ROPE_DIMS:]], axis=-1)
    kd = jnp.concatenate([apply_rope(kd[..., :ROPE_DIMS], cos, sin), kd[..., ROPE_DIMS:]], axis=-1)
    lam_init = 0.8 - 0.6 * math.exp(-0.3 * layer_idx)
    lp = lam_p.astype(F32)
    lam = jnp.exp(jnp.sum(lp[0] * lp[1])) - jnp.exp(jnp.sum(lp[2] * lp[3])) + lam_init
    vd = dv.reshape(B, S, DF_H, DF_DV)
    o = blocked_diff_attention(qd[:, :, 0::2], qd[:, :, 1::2], kd[:, :, 0::2], kd[:, :, 1::2], vd, lam, DF_DH ** -0.5)
    o = (rmsnorm(o, subln).astype(F32) * (1.0 - lam_init)).astype(h.dtype).reshape(B, S, DF_H * DF_DV)
    return jnp.concatenate([y_ret, o], axis=-1) @ w_out


def memory_xattn(h, mem_n, wq, wkv, qn, kn, wo):
    B, S, _ = h.shape
    q = rmsnorm((h @ wq).reshape(B, S, XA_H, XA_DH), qn)
    kv = (mem_n @ wkv).reshape(B, mem_n.shape[1], 2, XA_H, XA_DH)
    k = rmsnorm(kv[:, :, 0], kn)
    v = kv[:, :, 1]
    s = jnp.einsum('bqhd,bkhd->bhqk', q, k).astype(F32) * (XA_DH ** -0.5)
    p = jax.nn.softmax(s, axis=-1).astype(v.dtype)
    o = jnp.einsum('bhqk,bkhd->bqhd', p, v).reshape(B, S, XA_H * XA_DH)
    return o @ wo


def hier_moe(h, wg, we, w1, w3, w2):
    B, S, D = h.shape
    N = B * S
    xf = h.reshape(N, D)
    g_logits = (xf @ wg).astype(F32)
    g_prob = jax.nn.softmax(g_logits, axis=-1)
    _, g_idx = lax.top_k(g_logits, 1)
    p_grp = jnp.take_along_axis(g_prob, g_idx, axis=-1)
    e_logits = (xf @ we).astype(F32).reshape(N, MOE_GROUPS, MOE_PER_GROUP)
    e_in = jnp.take_along_axis(e_logits, g_idx[:, :, None], axis=1)[:, 0]
    top_v, top_i = lax.top_k(e_in, MOE_TOPK)
    top_w = jax.nn.softmax(top_v, axis=-1) * p_grp
    expert_id = g_idx * MOE_PER_GROUP + top_i
    A = N * MOE_TOPK
    flat_e = expert_id.reshape(A).astype(jnp.int32)
    flat_tok = jnp.repeat(jnp.arange(N, dtype=jnp.int32), MOE_TOPK)
    flat_w = top_w.reshape(A)
    order = jnp.argsort(flat_e)
    se = flat_e[order]
    counts = jnp.bincount(flat_e, length=MOE_EXPERTS).astype(jnp.int32)
    padded = ((counts + MOE_BLOCK - 1) // MOE_BLOCK) * MOE_BLOCK
    pad_end = jnp.cumsum(padded)
    pad_start = pad_end - padded
    start = jnp.cumsum(counts) - counts
    dest = pad_start[se] + jnp.arange(A, dtype=jnp.int32) - start[se]
    P = A + MOE_EXPERTS * MOE_BLOCK
    NB = P // MOE_BLOCK
    tok_sorted = flat_tok[order]
    buf_tok = jnp.full((P,), N, dtype=jnp.int32).at[dest].set(tok_sorted)
    blk_e = jnp.minimum(jnp.searchsorted(pad_end, jnp.arange(NB, dtype=jnp.int32) * MOE_BLOCK, side='right'),
                        MOE_EXPERTS - 1)
    x_pad = jnp.concatenate([xf, jnp.zeros((1, D), xf.dtype)], axis=0)
    xb = x_pad[buf_tok].reshape(NB, MOE_BLOCK, D)

    def expert_block(args):
        xblk, e = args
        return (jax.nn.silu(xblk @ w1[e]) * (xblk @ w3[e])) @ w2[e]

    yb = lax.map(expert_block, (xb, blk_e)).reshape(P, D)
    y_assign = (yb[dest].astype(F32) * flat_w[order][:, None]).astype(h.dtype)
    out = jnp.zeros((N, D), h.dtype).at[tok_sorted].add(y_assign)
    return out.reshape(B, S, D)


def trunk(x, mem, mem_norm, norm_mix, norm_mem, norm_ffn, ev_w_in, ev_gate_bias, ev_w_out, ml_head_norm,
          mla_qa_norm, mla_kva_norm, mla_w_uq, mla_w_ukv, mla_q_norm, mla_k_norm, od_w_in, od_w_out,
          ret_head_norm, df_lambda, df_q_norm, df_k_norm, df_subln, xa_w_q, xa_w_kv, xa_q_norm, xa_k_norm,
          xa_w_o, moe_w_group, moe_w_expert, moe_w1, moe_w3, moe_w2):
    mem_n = rmsnorm(mem, mem_norm)
    for layer in range(DEPTH):
        j = layer // 2
        h = rmsnorm(x, norm_mix[layer])
        if layer % 2 == 0:
            x = x + even_mixer(h, ev_w_in[j], ev_gate_bias[j], ev_w_out[j], ml_head_norm[j], mla_qa_norm[j],
                               mla_kva_norm[j], mla_w_uq[j], mla_w_ukv[j], mla_q_norm[j], mla_k_norm[j])
        else:
            x = x + odd_mixer(h, layer, od_w_in[j], od_w_out[j], ret_head_norm[j], df_lambda[j], df_q_norm[j],
                              df_k_norm[j], df_subln[j])
        h = rmsnorm(x, norm_mem[layer])
        x = x + memory_xattn(h, mem_n, xa_w_q[layer], xa_w_kv[layer], xa_q_norm[layer], xa_k_norm[layer], xa_w_o[layer])
        h = rmsnorm(x, norm_ffn[layer])
        x = x + hier_moe(h, moe_w_group[layer], moe_w_expert[layer], moe_w1[layer], moe_w3[layer], moe_w2[layer])
    return x


def setup_inputs(seed: int = 0) -> dict:
    key = jax.random.key(seed)
    ks = iter(jax.random.split(key, 48))
    nrm = lambda shape, scale: jax.random.normal(next(ks), shape, F32) * scale
    gain = lambda shape: 1.0 + 0.02 * jax.random.normal(next(ks), shape, F32)
    D = D_MODEL
    fb = jnp.linspace(ML_FGATE_BIAS_LO, ML_FGATE_BIAS_HI, ML_H, dtype=F32)
    zb = jnp.zeros((ML_H,), F32)
    gate_base = jnp.stack([zb, fb, zb, fb])
    return {
        'x_prompt': nrm((BATCH, SEQ, D), 1.0),
        'x_sample': nrm((DEC_BATCH, DEC_SEQ, D), 1.0),
        'mem_prompt': nrm((BATCH, N_MEM, D), 1.0),
        'mem_sample': nrm((DEC_BATCH, N_MEM, D), 1.0),
        'mem_norm': gain((D,)),
        'norm_mix': gain((DEPTH, D)),
        'norm_mem': gain((DEPTH, D)),
        'norm_ffn': gain((DEPTH, D)),
        'ev_w_in': nrm((N_EVEN, D, EV_IN), D ** -0.5),
        'ev_gate_bias': gate_base[None] + nrm((N_EVEN, 4, ML_H), 0.1),
        'ev_w_out': nrm((N_EVEN, EV_OUT, D), EV_OUT ** -0.5),
        'ml_head_norm': gain((N_EVEN, ML_W)),
        'mla_qa_norm': gain((N_EVEN, MLA_Q_LORA)),
        'mla_kva_norm': gain((N_EVEN, MLA_KV_LORA)),
        'mla_w_uq': nrm((N_EVEN, MLA_Q_LORA, MLA_H * (MLA_NOPE + MLA_ROPE)), MLA_Q_LORA ** -0.5),
        'mla_w_ukv': nrm((N_EVEN, MLA_KV_LORA, MLA_H * (MLA_NOPE + MLA_V)), MLA_KV_LORA ** -0.5),
        'mla_q_norm': gain((N_EVEN, MLA_NOPE + MLA_ROPE)),
        'mla_k_norm': gain((N_EVEN, MLA_NOPE + MLA_ROPE)),
        'od_w_in': nrm((N_ODD, D, OD_IN), D ** -0.5),
        'od_w_out': nrm((N_ODD, OD_OUT, D), OD_OUT ** -0.5),
        'ret_head_norm': gain((N_ODD, RET_H * RET_DV)),
        'df_lambda': nrm((N_ODD, 4, DF_DH), 0.1),
        'df_q_norm': gain((N_ODD, DF_DH)),
        'df_k_norm': gain((N_ODD, DF_DH)),
        'df_subln': gain((N_ODD, DF_DV)),
        'xa_w_q': nrm((DEPTH, D, XA_H * XA_DH), D ** -0.5),
        'xa_w_kv': nrm((DEPTH, D, 2 * XA_H * XA_DH), D ** -0.5),
        'xa_q_norm': gain((DEPTH, XA_DH)),
        'xa_k_norm': gain((DEPTH, XA_DH)),
        'xa_w_o': nrm((DEPTH, XA_H * XA_DH, D), (XA_H * XA_DH) ** -0.5),
        'moe_w_group': nrm((DEPTH, D, MOE_GROUPS), D ** -0.5),
        'moe_w_expert': nrm((DEPTH, D, MOE_EXPERTS), D ** -0.5),
        'moe_w1': nrm((DEPTH, MOE_EXPERTS, D, MOE_FF), D ** -0.5),
        'moe_w3': nrm((DEPTH, MOE_EXPERTS, D, MOE_FF), D ** -0.5),
        'moe_w2': nrm((DEPTH, MOE_EXPERTS, MOE_FF, D), MOE_FF ** -0.5),
    }


def reference(x_prompt, x_sample, mem_prompt, mem_sample, mem_norm, norm_mix, norm_mem, norm_ffn, ev_w_in,
              ev_gate_bias, ev_w_out, ml_head_norm, mla_qa_norm, mla_kva_norm, mla_w_uq, mla_w_ukv, mla_q_norm,
              mla_k_norm, od_w_in, od_w_out, ret_head_norm, df_lambda, df_q_norm, df_k_norm, df_subln, xa_w_q,
              xa_w_kv, xa_q_norm, xa_k_norm, xa_w_o, moe_w_group, moe_w_expert, moe_w1, moe_w3, moe_w2):
    weights = (mem_norm, norm_mix, norm_mem, norm_ffn, ev_w_in, ev_gate_bias, ev_w_out, ml_head_norm,
               mla_qa_norm, mla_kva_norm, mla_w_uq, mla_w_ukv, mla_q_norm, mla_k_norm, od_w_in, od_w_out,
               ret_head_norm, df_lambda, df_q_norm, df_k_norm, df_subln, xa_w_q, xa_w_kv, xa_q_norm, xa_k_norm,
               xa_w_o, moe_w_group, moe_w_expert, moe_w1, moe_w3, moe_w2)
    y_prompt = trunk(x_prompt, mem_prompt, *weights)
    y_sample = trunk(x_sample, mem_sample, *weights)
    return (y_prompt, y_sample)
```

```python
import functools
import math

import numpy as np
import jax
import jax.numpy as jnp
from jax import lax
from jax.experimental import pallas as pl
from jax.experimental.pallas import tpu as pltpu

F32 = jnp.float32
BF16 = jnp.bfloat16
I32 = jnp.int32

EPS = 1e-6
LANES = 128
CHUNK = 128

ML_H, ML_DH = 4, 128
ML_W = ML_H * ML_DH
MLA_H, MLA_NOPE, MLA_ROPE, MLA_V = 4, 128, 64, 128
MLA_Q_LORA, MLA_KV_LORA = 256, 128
MLA_THETA = 10000.0
MLA_DQ = MLA_NOPE + MLA_ROPE
MLA_DQP = 256
RET_H, RET_DK, RET_DV = 4, 128, 128
RET_THETA = 10000.0
RET_DECAY_EXP_FWD, RET_DECAY_EXP_BWD = 5.0, 5.5
DF_H, DF_DH = 4, 64
DF_DV = 2 * DF_DH
ROPE_THETA = 500000.0
ROPE_DIMS = DF_DH // 4
XA_H, XA_DH = 4, 128
MOE_GROUPS, MOE_PER_GROUP = 4, 8
MOE_EXPERTS = MOE_GROUPS * MOE_PER_GROUP
MOE_FF = 512
MOE_TB = 512

VMEM_LIMIT = 56 * 1024 * 1024


def _tile(n, pref):
    if n <= pref:
        return n
    t = pref - pref % 8
    while t >= 8:
        if n % t == 0:
            return t
        t -= 8
    return n


def _cparams(sem):
    return pltpu.CompilerParams(dimension_semantics=sem, vmem_limit_bytes=VMEM_LIMIT)


def _rms(x, gain):
    ms = jnp.mean(x * x, axis=-1, keepdims=True)
    return x * lax.rsqrt(ms + EPS) * gain


def _dot(a, b):
    return jnp.dot(a, b, preferred_element_type=F32)


def _dot_nt(a, b):
    return lax.dot_general(a, b, (((1,), (1,)), ((), ())), preferred_element_type=F32)


def _split_bf16(x):
    hi = x.astype(BF16)
    lo = (x - hi.astype(F32)).astype(BF16)
    return hi, lo


def _norm_matmul_body(x_ref, g_ref, w_ref, *out_refs, splits):
    h = _rms(x_ref[...], g_ref[...]).astype(BF16)
    off = 0
    for o_ref, n in zip(out_refs, splits):
        o_ref[...] = _dot(h, w_ref[:, off:off + n]).astype(o_ref.dtype)
        off += n


def norm_matmul(x2d, g, w, splits, dtypes, tm=512):
    n, d = x2d.shape
    tm = _tile(n, tm)
    ntot = w.shape[1]
    assert sum(splits) == ntot
    return pl.pallas_call(
        functools.partial(_norm_matmul_body, splits=tuple(splits)),
        grid=(n // tm,),
        in_specs=[pl.BlockSpec((tm, d), lambda i: (i, 0)),
                  pl.BlockSpec((1, d), lambda i: (0, 0)),
                  pl.BlockSpec((d, ntot), lambda i: (0, 0))],
        out_specs=[pl.BlockSpec((tm, s), lambda i: (i, 0)) for s in splits],
        out_shape=[jax.ShapeDtypeStruct((n, s), dt) for s, dt in zip(splits, dtypes)],
        compiler_params=_cparams(("parallel",)),
        name="norm_matmul",
    )(x2d, g.reshape(1, d).astype(F32), w)


def _log_sigmoid(x):
    return jnp.minimum(x, 0.0) - jnp.log1p(jnp.exp(-jnp.abs(x)))


def _mlstm_body(qf_ref, kf_ref, vf_ref, gf_ref, qb_ref, kb_ref, vb_ref, gb_ref, bias_ref,
                hf_ref, hb_ref, cn_scr, m_scr, *, nh, scale):
    L = CHUNK
    c = pl.program_id(1)

    @pl.when(c == 0)
    def _():
        cn_scr[...] = jnp.zeros_like(cn_scr)
        m_scr[...] = jnp.zeros_like(m_scr)

    row = lax.broadcasted_iota(I32, (L, L), 0)
    col = lax.broadcasted_iota(I32, (L, L), 1)
    ones_col = (lax.broadcasted_iota(I32, (L, LANES), 1) == 0).astype(BF16)
    dirs = ((qf_ref, kf_ref, vf_ref, gf_ref, hf_ref, col <= row, L - 1),
            (qb_ref, kb_ref, vb_ref, gb_ref, hb_ref, col >= row, 0))
    for d, (q_ref, k_ref, v_ref, g_ref, h_ref, mask, last) in enumerate(dirs):
        G = g_ref[0] + bias_ref[...]
        lf_hi, lf_lo = _split_bf16(_log_sigmoid(G))
        tri = mask.astype(BF16)
        Bc = _dot(tri, lf_hi) + _dot(tri, lf_lo)
        BcT = Bc.T
        GT = G.T
        for h in range(nh):
            ii = 2 * d * nh + h
            fi = (2 * d + 1) * nh + h
            ci = d * nh + h
            hs = slice(h * L, (h + 1) * L)
            q = q_ref[0, :, hs]
            k = k_ref[0, :, hs]
            v_aug = jnp.concatenate([v_ref[0, :, hs], ones_col], axis=-1)
            b_col, b_row = Bc[:, fi:fi + 1], BcT[fi:fi + 1, :]
            i_col, i_row = G[:, ii:ii + 1], GT[ii:ii + 1, :]
            b_last = Bc[last:last + 1, fi:fi + 1]
            m_prev = m_scr[ci:ci + 1, 0:1]
            cn = cn_scr[ci]

            log_d = jnp.where(mask, b_col - b_row + i_row, -jnp.inf)
            inter = b_col + m_prev
            m_t = jnp.maximum(inter, jnp.max(log_d, axis=-1, keepdims=True))
            a_inter = jnp.exp(inter - m_t)
            s = _dot_nt(q, k) * (jnp.exp(log_d - m_t) * scale)
            r = _dot(s.astype(BF16), v_aug) + a_inter * _dot(q, cn.astype(BF16))
            den = jnp.maximum(jnp.abs(r[:, L:L + 1]), jnp.exp(-m_t))
            h_ref[0, :, hs] = r[:, :L] / den

            w_log = b_last - b_col + i_col
            m_new = jnp.maximum(b_last + m_prev, jnp.max(w_log, axis=0, keepdims=True))
            dec = jnp.exp(b_last + m_prev - m_new)
            ws = jnp.exp(w_log - m_new) * scale
            kw_t = (k.astype(F32) * ws).T.astype(BF16)
            cn_scr[ci] = dec * cn + _dot(kw_t, v_aug)
            m_scr[ci:ci + 1, :] = jnp.broadcast_to(m_new, (1, LANES))


def mlstm_bidir(q, k, v, gates, bias_row):
    B, S, W = q.shape
    nh = W // CHUNK
    nc = S // CHUNK
    fwd = lambda b, c: (b, c, 0)
    bwd = lambda b, c: (b, nc - 1 - c, 0)
    blk = lambda w, im: pl.BlockSpec((1, CHUNK, w), im)
    return pl.pallas_call(
        functools.partial(_mlstm_body, nh=nh, scale=CHUNK ** -0.5),
        grid=(B, nc),
        in_specs=[blk(W, fwd), blk(W, fwd), blk(W, fwd), blk(LANES, fwd),
                  blk(W, bwd), blk(W, bwd), blk(W, bwd), blk(LANES, bwd),
                  pl.BlockSpec((1, LANES), lambda b, c: (0, 0))],
        out_specs=[blk(W, fwd), blk(W, bwd)],
        out_shape=[jax.ShapeDtypeStruct((B, S, W), F32)] * 2,
        scratch_shapes=[pltpu.VMEM((2 * nh, CHUNK, 2 * LANES), F32), pltpu.VMEM((2 * nh, LANES), F32)],
        compiler_params=_cparams(("parallel", "arbitrary")),
        name="mlstm_bidir",
    )(q, k, v, gates, q, k, v, gates, bias_row)


def _ret_body(qf_ref, kf_ref, vf_ref, cf_ref, sf_ref, qb_ref, kb_ref, cb_ref, sb_ref, vb_ref,
              mask_ref, qdec_ref, kdec_ref, yf_ref, yb_ref, st_scr, *, nh, scale, cdec):
    L = CHUNK
    c = pl.program_id(1)

    @pl.when(c == 0)
    def _():
        st_scr[...] = jnp.zeros_like(st_scr)

    def rope(x_ref, hs, cos_ref, sin_ref):
        x = x_ref[0, :, hs].astype(F32)
        return x * cos_ref[...] + pltpu.roll(x, L // 2, 1) * sin_ref[...]

    for h in range(nh):
        hs = slice(h * L, (h + 1) * L)
        q = rope(qf_ref, hs, cf_ref, sf_ref).astype(BF16)
        kf = rope(kf_ref, hs, cf_ref, sf_ref) * scale
        v = vf_ref[0, :, hs]
        st = st_scr[h]
        sc = _dot_nt(q, kf.astype(BF16)) * mask_ref[h]
        y = _dot(sc.astype(BF16), v) + _dot(q, st.astype(BF16)) * qdec_ref[h]
        yf_ref[0, :, hs] = y
        kd_t = (kf * kdec_ref[h]).T.astype(BF16)
        st_scr[h] = cdec[0][h] * st + _dot(kd_t, v)
        q = rope(qb_ref, hs, cb_ref, sb_ref).astype(BF16)
        kf = rope(kb_ref, hs, cb_ref, sb_ref) * scale
        v = vb_ref[0, :, hs]
        st = st_scr[nh + h]
        yb_ref[0, :, hs] = _dot(q, st.astype(BF16)) * qdec_ref[nh + h]
        kd_t = (kf * kdec_ref[nh + h]).T.astype(BF16)
        st_scr[nh + h] = cdec[1][h] * st + _dot(kd_t, v)


def _ret_constants():
    L = CHUNK
    heads = np.arange(RET_H, dtype=np.float64)
    lg_f = np.log1p(-np.exp2(-RET_DECAY_EXP_FWD - heads))
    lg_b = np.log1p(-np.exp2(-RET_DECAY_EXP_BWD - heads))
    pos = np.arange(L, dtype=np.float64)
    dist = pos[:, None] - pos[None, :]
    mask = np.where(dist[None] >= 0, np.exp(np.abs(dist)[None] * lg_f[:, None, None]),
                    np.exp(np.abs(dist)[None] * lg_b[:, None, None]))
    qdec = np.concatenate([np.exp((pos + 1.0)[None] * lg_f[:, None]), np.exp((L - pos)[None] * lg_b[:, None])])
    kdec = np.concatenate([np.exp((L - 1.0 - pos)[None] * lg_f[:, None]), np.exp(pos[None] * lg_b[:, None])])
    cdec = (tuple(float(x) for x in np.exp(L * lg_f)), tuple(float(x) for x in np.exp(L * lg_b)))
    as32 = lambda a: jnp.asarray(a, F32)
    return as32(mask), as32(qdec[:, :, None]), as32(kdec[:, :, None]), cdec


def retention_bidir(q, k, v, cos_t, sin_t):
    B, S, W = q.shape
    nh = W // CHUNK
    nc = S // CHUNK
    mask, qdec, kdec, cdec = _ret_constants()
    fwd = lambda b, c: (b, c, 0)
    bwd = lambda b, c: (b, nc - 1 - c, 0)
    blk = lambda im: pl.BlockSpec((1, CHUNK, W), im)
    tab_f = pl.BlockSpec((CHUNK, LANES), lambda b, c: (c, 0))
    tab_b = pl.BlockSpec((CHUNK, LANES), lambda b, c: (nc - 1 - c, 0))
    const = lambda shp: pl.BlockSpec(shp, lambda b, c: (0,) * len(shp))
    return pl.pallas_call(
        functools.partial(_ret_body, nh=nh, scale=RET_DK ** -0.5, cdec=cdec),
        grid=(B, nc),
        in_specs=[blk(fwd), blk(fwd), blk(fwd), tab_f, tab_f, blk(bwd), blk(bwd), tab_b, tab_b, blk(bwd),
                  const(mask.shape), const(qdec.shape), const(kdec.shape)],
        out_specs=[blk(fwd), blk(bwd)],
        out_shape=[jax.ShapeDtypeStruct((B, S, W), F32)] * 2,
        scratch_shapes=[pltpu.VMEM((2 * nh, CHUNK, CHUNK), F32)],
        compiler_params=_cparams(("parallel", "arbitrary")),
        name="retention_bidir",
    )(q, k, v, cos_t, sin_t, q, k, cos_t, sin_t, v, mask, qdec, kdec)


def _flash_body(q_ref, k_ref, v_ref, o_ref, m_scr, l_scr, acc_scr):
    j = pl.program_id(3)

    @pl.when(j == 0)
    def _():
        m_scr[...] = jnp.full_like(m_scr, -jnp.inf)
        l_scr[...] = jnp.zeros_like(l_scr)
        acc_scr[...] = jnp.zeros_like(acc_scr)

    s = _dot_nt(q_ref[0], k_ref[0])
    m_prev = m_scr[...]
    m_new = jnp.maximum(m_prev, jnp.max(s, axis=-1, keepdims=True))
    alpha = jnp.exp(m_prev - m_new)
    p = jnp.exp(s - m_new)
    l_scr[...] = alpha * l_scr[...] + jnp.sum(p, axis=-1, keepdims=True)
    acc_scr[...] = alpha * acc_scr[...] + _dot(p.astype(BF16), v_ref[0])
    m_scr[...] = m_new

    @pl.when(j == pl.num_programs(3) - 1)
    def _():
        o_ref[0] = (acc_scr[...] / l_scr[...]).astype(o_ref.dtype)


def flash_attention(q, k, v, nheads, kv_group, dq, dv, tq=512, tk=512):
    B, S, _ = q.shape
    T = k.shape[1]
    tq = _tile(S, tq)
    tk = _tile(T, tk)
    return pl.pallas_call(
        _flash_body,
        grid=(B, nheads, S // tq, T // tk),
        in_specs=[pl.BlockSpec((1, tq, dq), lambda b, h, i, j: (b, i, h)),
                  pl.BlockSpec((1, tk, dq), lambda b, h, i, j: (b, j, h // kv_group)),
                  pl.BlockSpec((1, tk, dv), lambda b, h, i, j: (b, j, h // kv_group))],
        out_specs=pl.BlockSpec((1, tq, dv), lambda b, h, i, j: (b, i, h)),
        out_shape=jax.ShapeDtypeStruct((B, S, nheads * dv), BF16),
        scratch_shapes=[pltpu.VMEM((tq, 1), F32), pltpu.VMEM((tq, 1), F32), pltpu.VMEM((tq, dv), F32)],
        compiler_params=_cparams(("parallel", "parallel", "parallel", "arbitrary")),
        name="flash_attention",
    )(q, k, v)


def _mla_prep_body(cq_ref, ckv_ref, kpe_ref, cos_ref, sin_ref, qan_ref, kvan_ref, wuq_ref, wuk_ref, wuv_ref,
                   qn_ref, kn_ref, q_out, k_out, v_out, *, scale):
    cos, sin = cos_ref[...], sin_ref[...]

    def rot(x):
        return x * cos + pltpu.roll(x, LANES // 2, 1) * sin

    cqn = _rms(cq_ref[0].astype(F32), qan_ref[...]).astype(BF16)
    uq = _dot(cqn, wuq_ref[...])
    ckvn = _rms(ckv_ref[0].astype(F32), kvan_ref[...]).astype(BF16)
    uk = _dot(ckvn, wuk_ref[...])
    v_out[0] = _dot(ckvn, wuv_ref[...]).astype(v_out.dtype)
    kpe = kpe_ref[0].astype(F32)
    kpe_ss = jnp.sum(kpe * kpe, axis=-1, keepdims=True)
    qn, kn = qn_ref[...], kn_ref[...]
    for h in range(MLA_H):
        lo = h * MLA_DQP
        blk = uq[:, lo:lo + MLA_DQP]
        r = lax.rsqrt(jnp.sum(blk * blk, axis=-1, keepdims=True) * (1.0 / MLA_DQ) + EPS) * scale
        y = blk * r * qn
        q_out[0, :, lo:lo + LANES] = y[:, :LANES].astype(q_out.dtype)
        q_out[0, :, lo + LANES:lo + MLA_DQP] = rot(y[:, LANES:]).astype(q_out.dtype)
        kh = uk[:, h * LANES:(h + 1) * LANES]
        r = lax.rsqrt((jnp.sum(kh * kh, axis=-1, keepdims=True) + kpe_ss) * (1.0 / MLA_DQ) + EPS)
        k_out[0, :, lo:lo + LANES] = (kh * r * kn[:, :LANES]).astype(k_out.dtype)
        k_out[0, :, lo + LANES:lo + MLA_DQP] = rot(kpe * r * kn[:, LANES:]).astype(k_out.dtype)


def _spread_rope64(a, axis):
    x1, x2 = jnp.split(a, 2, axis=axis)
    z = jnp.zeros_like(x1)
    return jnp.concatenate([x1, z, x2, z], axis=axis)


def mla_prep(cq, ckv, kpe, w_uq, w_ukv, qa_n, kva_n, q_n, k_n, tm=512):
    B, S, _ = cq.shape
    tm = _tile(S, tm)
    half = MLA_ROPE // 2
    inv = 1.0 / (MLA_THETA ** (jnp.arange(0, MLA_ROPE, 2, dtype=F32) / MLA_ROPE))
    ang = jnp.arange(S, dtype=F32)[:, None] * inv[None, :]
    c, s = jnp.cos(ang), jnp.sin(ang)
    z = jnp.zeros((S, half), F32)
    cos_t = jnp.concatenate([c, z, c, z], axis=1)
    sin_t = jnp.concatenate([-s, z, s, z], axis=1)
    wq = w_uq.reshape(MLA_Q_LORA, MLA_H, MLA_DQ)
    wq = jnp.concatenate([wq[..., :MLA_NOPE], _spread_rope64(wq[..., MLA_NOPE:], 2)], axis=2)
    wq = wq.reshape(MLA_Q_LORA, MLA_H * MLA_DQP).astype(BF16)
    wkv = w_ukv.reshape(MLA_KV_LORA, MLA_H, MLA_NOPE + MLA_V)
    wuk = wkv[..., :MLA_NOPE].reshape(MLA_KV_LORA, MLA_H * MLA_NOPE).astype(BF16)
    wuv = wkv[..., MLA_NOPE:].reshape(MLA_KV_LORA, MLA_H * MLA_V).astype(BF16)
    pad_gain = lambda g: jnp.concatenate([g[:MLA_NOPE], _spread_rope64(g[MLA_NOPE:], 0)]).reshape(1, MLA_DQP).astype(F32)
    tok = lambda w: pl.BlockSpec((1, tm, w), lambda b, i: (b, i, 0))
    tab = pl.BlockSpec((tm, LANES), lambda b, i: (i, 0))
    const = lambda a: pl.BlockSpec(a.shape, lambda b, i: (0, 0))
    args = [cq, ckv, kpe, cos_t, sin_t, qa_n.reshape(1, -1).astype(F32), kva_n.reshape(1, -1).astype(F32),
            wq, wuk, wuv, pad_gain(q_n), pad_gain(k_n)]
    return pl.pallas_call(
        functools.partial(_mla_prep_body, scale=MLA_DQ ** -0.5),
        grid=(B, S // tm),
        in_specs=[tok(MLA_Q_LORA), tok(MLA_KV_LORA), tok(LANES), tab, tab] + [const(a) for a in args[5:]],
        out_specs=[tok(MLA_H * MLA_DQP), tok(MLA_H * MLA_DQP), tok(MLA_H * MLA_V)],
        out_shape=[jax.ShapeDtypeStruct((B, S, MLA_H * MLA_DQP), BF16),
                   jax.ShapeDtypeStruct((B, S, MLA_H * MLA_DQP), BF16),
                   jax.ShapeDtypeStruct((B, S, MLA_H * MLA_V), BF16)],
        compiler_params=_cparams(("parallel", "parallel")),
        name="mla_prep",
    )(*args)


def _diff_prep_body(dq_ref, dk_ref, cos_ref, s1_ref, s2_ref, qn_ref, kn_ref, seg_ref, q_out, k_out, *, scale):
    cos, s1, s2 = cos_ref[...], s1_ref[...], s2_ref[...]
    lane = lax.broadcasted_iota(I32, (1, LANES), 1)

    def norm_rot(x_ref, gain):
        x = x_ref[0].astype(F32)
        hi, lo = _split_bf16(x * x)
        ms = (_dot(hi, seg_ref[...]) + _dot(lo, seg_ref[...])) * (1.0 / DF_DH)
        y = x * lax.rsqrt(ms + EPS) * gain
        out = []
        for b in range(DF_H):
            yb = y[:, b * LANES:(b + 1) * LANES]
            out.append(yb * cos + pltpu.roll(yb, LANES - ROPE_DIMS // 2, 1) * s1
                       + pltpu.roll(yb, ROPE_DIMS // 2, 1) * s2)
        return out

    qs = norm_rot(dq_ref, qn_ref[...])
    ks = norm_rot(dk_ref, kn_ref[...])
    for b in range(DF_H):
        qb = qs[b] * scale
        q_out[0, :, (2 * b) * LANES:(2 * b + 1) * LANES] = jnp.where(lane < DF_DH, qb, 0.0).astype(q_out.dtype)
        q_out[0, :, (2 * b + 1) * LANES:(2 * b + 2) * LANES] = jnp.where(lane >= DF_DH, qb, 0.0).astype(q_out.dtype)
        k_out[0, :, b * LANES:(b + 1) * LANES] = ks[b].astype(k_out.dtype)


def diff_prep(dq, dk, q_n, k_n, tm=512):
    B, S, W = dq.shape
    tm = _tile(S, tm)
    half = ROPE_DIMS // 2
    inv = 1.0 / (ROPE_THETA ** (jnp.arange(0, ROPE_DIMS, 2, dtype=F32) / ROPE_DIMS))
    ang = jnp.arange(S, dtype=F32)[:, None] * inv[None, :]
    c, s = jnp.cos(ang), jnp.sin(ang)
    rest = DF_DH - ROPE_DIMS
    seg_c = jnp.concatenate([c, c, jnp.ones((S, rest), F32)], axis=1)
    seg_s1 = jnp.concatenate([-s, jnp.zeros((S, half + rest), F32)], axis=1)
    seg_s2 = jnp.concatenate([jnp.zeros((S, half), F32), s, jnp.zeros((S, rest), F32)], axis=1)
    two = lambda a: jnp.concatenate([a, a], axis=1)
    seg = jnp.asarray(np.kron(np.eye(W // DF_DH), np.ones((DF_DH, DF_DH))), BF16)
    gain = lambda g: jnp.tile(g.astype(F32), W // DF_DH).reshape(1, W)
    tok = lambda w: pl.BlockSpec((1, tm, w), lambda b, i: (b, i, 0))
    tab = pl.BlockSpec((tm, LANES), lambda b, i: (i, 0))
    const = lambda shp: pl.BlockSpec(shp, lambda b, i: (0, 0))
    return pl.pallas_call(
        functools.partial(_diff_prep_body, scale=DF_DH ** -0.5),
        grid=(B, S // tm),
        in_specs=[tok(W), tok(W), tab, tab, tab, const((1, W)), const((1, W)), const((W, W))],
        out_specs=[tok(2 * W), tok(W)],
        out_shape=[jax.ShapeDtypeStruct((B, S, 2 * W), BF16), jax.ShapeDtypeStruct((B, S, W), BF16)],
        compiler_params=_cparams(("parallel", "parallel")),
        name="diff_prep",
    )(dq, dk, two(seg_c), two(seg_s1), two(seg_s2), gain(q_n), gain(k_n), seg)


def _head_rms(y, gain, nh):
    out = []
    for h in range(nh):
        hs = slice(h * LANES, (h + 1) * LANES)
        out.append(_rms(y[:, hs], gain[:, hs]))
    return jnp.concatenate(out, axis=-1)


def _even_out_body(x_ref, hf_ref, hb_ref, mo_ref, oa_ref, hn_ref, w_ref, o_ref):
    hm = _head_rms(hf_ref[...] + hb_ref[...], hn_ref[...], ML_H)
    hm = (hm * jax.nn.sigmoid(mo_ref[...].astype(F32))).astype(BF16)
    o_ref[...] = x_ref[...] + _dot(hm, w_ref[:ML_W, :]) + _dot(oa_ref[...], w_ref[ML_W:, :])


def _odd_out_body(x_ref, yf_ref, yb_ref, rg_ref, od_ref, hn_ref, sub_ref, lam_ref, w_ref, o_ref, *, lam_init):
    yr = _head_rms(yf_ref[...] + yb_ref[...], hn_ref[...], RET_H)
    yr = (yr * jax.nn.silu(rg_ref[...].astype(F32))).astype(BF16)
    lp = lam_ref[...]
    lam = (jnp.exp(jnp.sum(lp[0:1] * lp[1:2], axis=-1, keepdims=True))
           - jnp.exp(jnp.sum(lp[2:3] * lp[3:4], axis=-1, keepdims=True)) + lam_init)
    od = od_ref[...].astype(F32)
    outs = []
    for h in range(DF_H):
        o1 = od[:, (2 * h) * LANES:(2 * h + 1) * LANES]
        o2 = od[:, (2 * h + 1) * LANES:(2 * h + 2) * LANES]
        outs.append(_rms(o1 - lam * o2, sub_ref[...]) * (1.0 - lam_init))
    o = jnp.concatenate(outs, axis=-1).astype(BF16)
    nr = RET_H * RET_DV
    o_ref[...] = x_ref[...] + _dot(yr, w_ref[:nr, :]) + _dot(o, w_ref[nr:, :])


def _rowwise_call(body, x2d, row_args, const_args, name, tm=512):
    n, d = x2d.shape
    tm = _tile(n, tm)
    rows = [x2d] + list(row_args)
    return pl.pallas_call(
        body,
        grid=(n // tm,),
        in_specs=[pl.BlockSpec((tm, a.shape[1]), lambda i: (i, 0)) for a in rows]
        + [pl.BlockSpec(a.shape, lambda i: (0, 0)) for a in const_args],
        out_specs=pl.BlockSpec((tm, d), lambda i: (i, 0)),
        out_shape=jax.ShapeDtypeStruct((n, d), F32),
        compiler_params=_cparams(("parallel",)),
        name=name,
    )(*rows, *const_args)


def _xattn_body(x_ref, k_ref, v_ref, g_ref, wq_ref, qn_ref, kn_ref, wo_ref, o_ref, kn_scr, *, scale):
    @pl.when(pl.program_id(1) == 0)
    def _():
        kk = k_ref[0].astype(F32)
        for h in range(XA_H):
            hs = slice(h * XA_DH, (h + 1) * XA_DH)
            kn_scr[:, hs] = _rms(kk[:, hs], kn_ref[...]).astype(BF16)

    x = x_ref[0]
    q = _dot(_rms(x, g_ref[...]).astype(BF16), wq_ref[...])
    outs = []
    for h in range(XA_H):
        hs = slice(h * XA_DH, (h + 1) * XA_DH)
        qh = (_rms(q[:, hs], qn_ref[...]) * scale).astype(BF16)
        s = _dot_nt(qh, kn_scr[:, hs])
        p = jnp.exp(s - jnp.max(s, axis=-1, keepdims=True))
        o = _dot(p.astype(BF16), v_ref[0, :, hs]) / jnp.sum(p, axis=-1, keepdims=True)
        outs.append(o.astype(BF16))
    o_ref[0] = x + _dot(jnp.concatenate(outs, axis=-1), wo_ref[...])


def memory_xattn(x, k, v, g, wq, qn, kn, wo, tm=512):
    B, S, D = x.shape
    M = k.shape[1]
    W = XA_H * XA_DH
    tm = _tile(S, tm)
    const = lambda a: pl.BlockSpec(a.shape, lambda b, i: (0, 0))
    consts = [g.reshape(1, D).astype(F32), wq, qn.reshape(1, XA_DH).astype(F32), kn.reshape(1, XA_DH).astype(F32), wo]
    return pl.pallas_call(
        functools.partial(_xattn_body, scale=XA_DH ** -0.5),
        grid=(B, S // tm),
        in_specs=[pl.BlockSpec((1, tm, D), lambda b, i: (b, i, 0)),
                  pl.BlockSpec((1, M, W), lambda b, i: (b, 0, 0)),
                  pl.BlockSpec((1, M, W), lambda b, i: (b, 0, 0))] + [const(a) for a in consts],
        out_specs=pl.BlockSpec((1, tm, D), lambda b, i: (b, i, 0)),
        out_shape=jax.ShapeDtypeStruct((B, S, D), F32),
        scratch_shapes=[pltpu.VMEM((M, W), BF16)],
        compiler_params=_cparams(("parallel", "arbitrary")),
        name="memory_xattn",
    )(x, k, v, *consts)


PK_E1, PK_E2, PK_R1, PK_R2, PK_W1, PK_W2 = range(6)


def _router_body(x_ref, g_ref, whi_ref, wlo_ref, h_ref, pk_ref, cnt_ref, carry_scr):
    i = pl.program_id(0)

    @pl.when(i == 0)
    def _():
        carry_scr[...] = jnp.zeros_like(carry_scr)

    h = _rms(x_ref[...], g_ref[...])
    h_ref[...] = h
    tm = h.shape[0]
    h_hi, h_lo = _split_bf16(h)
    logits = _dot(h_hi, whi_ref[...]) + (_dot(h_lo, whi_ref[...]) + _dot(h_hi, wlo_ref[...]))
    lane = lax.broadcasted_iota(I32, (tm, LANES), 1)
    neg = -jnp.inf
    big = 4 * LANES

    def top(vals):
        v = jnp.max(vals, axis=-1, keepdims=True)
        idx = jnp.min(jnp.where(vals == v, lane, big), axis=-1, keepdims=True)
        return v, idx

    gl = jnp.where((lane >= MOE_EXPERTS) & (lane < MOE_EXPERTS + MOE_GROUPS), logits, neg)
    gmax, glane = top(gl)
    p_grp = 1.0 / jnp.sum(jnp.exp(gl - gmax), axis=-1, keepdims=True)
    gidx = glane - MOE_EXPERTS
    grp_of_lane = lax.shift_right_logical(lane, int(math.log2(MOE_PER_GROUP)))
    el = jnp.where((lane < MOE_EXPERTS) & (grp_of_lane == gidx), logits, neg)
    v1, i1 = top(el)
    v2, i2 = top(jnp.where(lane == i1, neg, el))
    e21 = jnp.exp(v2 - v1)
    w1 = p_grp / (1.0 + e21)
    w2 = w1 * e21
    twohot = ((lane == i1) | (lane == i2)).astype(F32)
    r = lax.broadcasted_iota(I32, (tm, tm), 0)
    c = lax.broadcasted_iota(I32, (tm, tm), 1)
    before = _dot((c < r).astype(BF16), twohot.astype(BF16)) + carry_scr[...]
    r1 = jnp.sum(jnp.where(lane == i1, before, 0.0), axis=-1, keepdims=True)
    r2 = jnp.sum(jnp.where(lane == i2, before, 0.0), axis=-1, keepdims=True)
    carry_scr[...] += jnp.sum(twohot, axis=0, keepdims=True)
    cnt_ref[...] = carry_scr[...]
    pk = jnp.zeros((tm, LANES), F32)
    for ln, val in ((PK_E1, i1.astype(F32)), (PK_E2, i2.astype(F32)), (PK_R1, r1), (PK_R2, r2),
                    (PK_W1, w1), (PK_W2, w2)):
        pk = jnp.where(lane == ln, val, pk)
    pk_ref[...] = pk


def moe_router(x2d, g, wg, we, tm=512):
    n, d = x2d.shape
    tm = _tile(n, tm)
    wr = jnp.concatenate([we, wg, jnp.zeros((d, LANES - MOE_EXPERTS - MOE_GROUPS), F32)], axis=1)
    w_hi, w_lo = _split_bf16(wr)
    return pl.pallas_call(
        _router_body,
        grid=(n // tm,),
        in_specs=[pl.BlockSpec((tm, d), lambda i: (i, 0)), pl.BlockSpec((1, d), lambda i: (0, 0)),
                  pl.BlockSpec((d, LANES), lambda i: (0, 0)), pl.BlockSpec((d, LANES), lambda i: (0, 0))],
        out_specs=[pl.BlockSpec((tm, d), lambda i: (i, 0)), pl.BlockSpec((tm, LANES), lambda i: (i, 0)),
                   pl.BlockSpec((1, LANES), lambda i: (0, 0))],
        out_shape=[jax.ShapeDtypeStruct((n, d), F32), jax.ShapeDtypeStruct((n, LANES), F32),
                   jax.ShapeDtypeStruct((1, LANES), F32)],
        scratch_shapes=[pltpu.VMEM((1, LANES), F32)],
        compiler_params=_cparams(("arbitrary",)),
        name="moe_router",
    )(x2d, g.reshape(1, d).astype(F32), w_hi, w_lo)


def _row_copy(src, s, dst, d, sem):
    return pltpu.make_async_copy(src.at[pl.ds(s, 1)], dst.at[pl.ds(d, 1)], sem)


def _dispatch_body(ps_ref, idx_ref, h_hbm, xb_in, xb_hbm, sem, *, tt):
    del xb_in
    base = pl.program_id(0) * tt

    def slots(t):
        return (ps_ref[idx_ref[0, 0, t]] + idx_ref[0, 2, t], ps_ref[idx_ref[0, 1, t]] + idx_ref[0, 3, t])

    def issue(t, carry):
        d1, d2 = slots(t)
        _row_copy(h_hbm, base + t, xb_hbm, d1, sem).start()
        _row_copy(h_hbm, base + t, xb_hbm, d2, sem).start()
        return carry

    def drain(t, carry):
        d1, d2 = slots(t)
        _row_copy(h_hbm, base + t, xb_hbm, d1, sem).wait()
        _row_copy(h_hbm, base + t, xb_hbm, d2, sem).wait()
        return carry

    lax.fori_loop(0, tt, issue, 0)
    lax.fori_loop(0, tt, drain, 0)


def moe_dispatch(h2d, idx, pad_start, nslots, tt):
    n, d = h2d.shape
    xb0 = jnp.zeros((nslots, d), F32)
    return pl.pallas_call(
        functools.partial(_dispatch_body, tt=tt),
        grid_spec=pltpu.PrefetchScalarGridSpec(
            num_scalar_prefetch=1,
            grid=(n // tt,),
            in_specs=[pl.BlockSpec((1, 4, tt), lambda i, ps: (i, 0, 0), memory_space=pltpu.SMEM),
                      pl.BlockSpec(memory_space=pl.ANY), pl.BlockSpec(memory_space=pl.ANY)],
            out_specs=pl.BlockSpec(memory_space=pl.ANY),
            scratch_shapes=[pltpu.SemaphoreType.DMA(())]),
        out_shape=jax.ShapeDtypeStruct((nslots, d), F32),
        input_output_aliases={3: 0},
        compiler_params=pltpu.CompilerParams(dimension_semantics=("arbitrary",), has_side_effects=True),
        name="moe_dispatch",
    )(pad_start, idx, h2d, xb0)


def _expert_body(be_ref, nused_ref, xb_ref, w1_ref, w3_ref, w2_ref, yb_ref):
    i = pl.program_id(0)

    @pl.when(i < nused_ref[0])
    def _():
        x = xb_ref[...].astype(BF16)
        a = _dot(x, w1_ref[0])
        b = _dot(x, w3_ref[0])
        yb_ref[...] = _dot((jax.nn.silu(a) * b).astype(BF16), w2_ref[0])

    @pl.when(i >= nused_ref[0])
    def _():
        yb_ref[...] = jnp.zeros_like(yb_ref)


def moe_experts(xb, blk_e, nused, w1, w3, w2):
    p, d = xb.shape
    ff = w1.shape[2]
    return pl.pallas_call(
        _expert_body,
        grid_spec=pltpu.PrefetchScalarGridSpec(
            num_scalar_prefetch=2,
            grid=(p // MOE_TB,),
            in_specs=[pl.BlockSpec((MOE_TB, d), lambda i, be, nu: (i, 0)),
                      pl.BlockSpec((1, d, ff), lambda i, be, nu: (be[i], 0, 0)),
                      pl.BlockSpec((1, d, ff), lambda i, be, nu: (be[i], 0, 0)),
                      pl.BlockSpec((1, ff, d), lambda i, be, nu: (be[i], 0, 0))],
            out_specs=pl.BlockSpec((MOE_TB, d), lambda i, be, nu: (i, 0))),
        out_shape=jax.ShapeDtypeStruct((p, d), F32),
        compiler_params=_cparams(("arbitrary",)),
        name="moe_experts",
    )(blk_e, nused, xb, w1, w3, w2)


def _combine_body(ps_ref, idx_ref, x_ref, pk_ref, yb_hbm, o_ref, y1_scr, y2_scr, sem, *, tt):
    def slots(t):
        return (ps_ref[idx_ref[0, 0, t]] + idx_ref[0, 2, t], ps_ref[idx_ref[0, 1, t]] + idx_ref[0, 3, t])

    def issue(t, carry):
        d1, d2 = slots(t)
        _row_copy(yb_hbm, d1, y1_scr, t, sem).start()
        _row_copy(yb_hbm, d2, y2_scr, t, sem).start()
        return carry

    def drain(t, carry):
        d1, d2 = slots(t)
        _row_copy(yb_hbm, d1, y1_scr, t, sem).wait()
        _row_copy(yb_hbm, d2, y2_scr, t, sem).wait()
        return carry

    lax.fori_loop(0, tt, issue, 0)
    lax.fori_loop(0, tt, drain, 0)
    pk = pk_ref[...]
    o_ref[...] = (x_ref[...] + pk[:, PK_W1:PK_W1 + 1] * y1_scr[...]) + pk[:, PK_W2:PK_W2 + 1] * y2_scr[...]


def moe_combine(x2d, pk, idx, pad_start, yb, tt):
    n, d = x2d.shape
    return pl.pallas_call(
        functools.partial(_combine_body, tt=tt),
        grid_spec=pltpu.PrefetchScalarGridSpec(
            num_scalar_prefetch=1,
            grid=(n // tt,),
            in_specs=[pl.BlockSpec((1, 4, tt), lambda i, ps: (i, 0, 0), memory_space=pltpu.SMEM),
                      pl.BlockSpec((tt, d), lambda i, ps: (i, 0)),
                      pl.BlockSpec((tt, LANES), lambda i, ps: (i, 0)),
                      pl.BlockSpec(memory_space=pl.ANY)],
            out_specs=pl.BlockSpec((tt, d), lambda i, ps: (i, 0)),
            scratch_shapes=[pltpu.VMEM((tt, d), F32), pltpu.VMEM((tt, d), F32), pltpu.SemaphoreType.DMA(())]),
        out_shape=jax.ShapeDtypeStruct((n, d), F32),
        compiler_params=_cparams(("arbitrary",)),
        name="moe_combine",
    )(pad_start, idx, x2d, pk, yb)


def hier_moe_residual(x2d, g, wg, we, w1, w3, w2):
    n, d = x2d.shape
    h, pk, cnt = moe_router(x2d, g, wg, we)
    tt = _tile(n, 512)
    idx = pk[:, :4].astype(I32).reshape(n // tt, tt, 4).transpose(0, 2, 1)
    counts = cnt[0, :MOE_EXPERTS].astype(I32)
    padded = ((counts + MOE_TB - 1) // MOE_TB) * MOE_TB
    pad_end = jnp.cumsum(padded)
    pad_start = (pad_end - padded).astype(I32)
    nslots = 2 * n + MOE_EXPERTS * MOE_TB
    nblk = nslots // MOE_TB
    blk_e = jnp.searchsorted(pad_end, jnp.arange(nblk, dtype=I32) * MOE_TB, side="right")
    last_e = jnp.max(jnp.where(counts > 0, jnp.arange(MOE_EXPERTS, dtype=I32), 0))
    blk_e = jnp.minimum(blk_e, last_e).astype(I32)
    nused = (pad_end[-1:] // MOE_TB).astype(I32)
    xb = moe_dispatch(h, idx, pad_start, nslots, tt)
    yb = moe_experts(xb, blk_e, nused, w1, w3, w2)
    return moe_combine(x2d, pk, idx, pad_start, yb, tt)


def _even_layer(x, norm_g, w_in, gate_bias, w_out, ml_hn, qa_n, kva_n, w_uq, w_ukv, q_n, k_n):
    B, S, D = x.shape
    n = B * S
    ng = 4 * ML_H
    cols = np.cumsum([0, ML_W, ML_W, ML_W, ML_W, ng, MLA_Q_LORA, MLA_KV_LORA, MLA_ROPE])
    part = lambda j: w_in[:, cols[j]:cols[j + 1]]
    w_g = jnp.concatenate([part(4), jnp.zeros((D, LANES - ng), F32)], axis=1)
    w_cat = jnp.concatenate([part(0), part(1), part(2), part(3), part(5), part(6), _spread_rope64(part(7), 1), w_g],
                            axis=1).astype(BF16)
    splits = [ML_W, ML_W, ML_W, ML_W, MLA_Q_LORA, MLA_KV_LORA, LANES, LANES]
    mq, mk, mv, mo, cq, ckv, kpe, mg = norm_matmul(x.reshape(n, D), norm_g, w_cat, splits, [BF16] * 7 + [F32])
    r3 = lambda a: a.reshape(B, S, a.shape[-1])
    bias_row = jnp.concatenate([gate_bias.reshape(ng).astype(F32), jnp.zeros((LANES - ng,), F32)]).reshape(1, LANES)
    hf, hb = mlstm_bidir(r3(mq), r3(mk), r3(mv), r3(mg), bias_row)
    qf, kf, vv = mla_prep(r3(cq), r3(ckv), r3(kpe), w_uq, w_ukv, qa_n, kva_n, q_n, k_n)
    o_mla = flash_attention(qf, kf, vv, MLA_H, 1, MLA_DQP, MLA_V)
    out = _rowwise_call(_even_out_body, x.reshape(n, D),
                        [hf.reshape(n, ML_W), hb.reshape(n, ML_W), mo, o_mla.reshape(n, MLA_H * MLA_V)],
                        [ml_hn.reshape(1, ML_W).astype(F32), w_out.astype(BF16)], "even_out")
    return out.reshape(B, S, D)


def _odd_layer(x, layer_idx, norm_g, w_in, w_out, ret_hn, lam_p, q_n, k_n, subln):
    B, S, D = x.shape
    n = B * S
    W = RET_H * RET_DK
    rq, rk, rv, rg, dq, dk, dv = norm_matmul(x.reshape(n, D), norm_g, w_in.astype(BF16), [W] * 7, [BF16] * 7)
    r3 = lambda a: a.reshape(B, S, a.shape[-1])
    inv = 1.0 / (RET_THETA ** (jnp.arange(0, RET_DK, 2, dtype=F32) / RET_DK))
    ang = jnp.arange(S, dtype=F32)[:, None] * inv[None, :]
    c, s = jnp.cos(ang), jnp.sin(ang)
    yf, yb = retention_bidir(r3(rq), r3(rk), r3(rv), jnp.concatenate([c, c], axis=1), jnp.concatenate([-s, s], axis=1))
    qd, kd = diff_prep(r3(dq), r3(dk), q_n, k_n)
    od = flash_attention(qd, kd, r3(dv), 2 * DF_H, 2, LANES, DF_DV)
    lam_init = 0.8 - 0.6 * math.exp(-0.3 * layer_idx)
    lam_rows = jnp.concatenate([lam_p.astype(F32), jnp.zeros((4, LANES - DF_DH), F32)], axis=1)
    out = _rowwise_call(functools.partial(_odd_out_body, lam_init=lam_init), x.reshape(n, D),
                        [yf.reshape(n, W), yb.reshape(n, W), rg, od.reshape(n, 2 * W)],
                        [ret_hn.reshape(1, W).astype(F32), subln.reshape(1, DF_DV).astype(F32), lam_rows,
                         w_out.astype(BF16)], "odd_out")
    return out.reshape(B, S, D)


def _trunk(x, mem, mem_norm, norm_mix, norm_mem, norm_ffn, ev_w_in, ev_gate_bias, ev_w_out, ml_head_norm,
           mla_qa_norm, mla_kva_norm, mla_w_uq, mla_w_ukv, mla_q_norm, mla_k_norm, od_w_in, od_w_out,
           ret_head_norm, df_lambda, df_q_norm, df_k_norm, df_subln, xa_w_q, xa_w_kv, xa_q_norm, xa_k_norm,
           xa_w_o, moe_w_group, moe_w_expert, moe_w1, moe_w3, moe_w2):
    B, S, D = x.shape
    depth = norm_mix.shape[0]
    M = mem.shape[1]
    W = XA_H * XA_DH
    for layer in range(depth):
        j = layer // 2
        if layer % 2 == 0:
            x = _even_layer(x, norm_mix[layer], ev_w_in[j], ev_gate_bias[j], ev_w_out[j], ml_head_norm[j],
                            mla_qa_norm[j], mla_kva_norm[j], mla_w_uq[j], mla_w_ukv[j], mla_q_norm[j], mla_k_norm[j])
        else:
            x = _odd_layer(x, layer, norm_mix[layer], od_w_in[j], od_w_out[j], ret_head_norm[j], df_lambda[j],
                           df_q_norm[j], df_k_norm[j], df_subln[j])
        mk, mv = norm_matmul(mem.reshape(B * M, D), mem_norm, xa_w_kv[layer].astype(BF16), [W, W], [BF16, BF16])
        x = memory_xattn(x, mk.reshape(B, M, W), mv.reshape(B, M, W), norm_mem[layer], xa_w_q[layer].astype(BF16),
                         xa_q_norm[layer], xa_k_norm[layer], xa_w_o[layer].astype(BF16))
        x = hier_moe_residual(x.reshape(B * S, D), norm_ffn[layer], moe_w_group[layer], moe_w_expert[layer],
                              moe_w1[layer], moe_w3[layer], moe_w2[layer]).reshape(B, S, D)
    return x


def kernel(x_prompt, x_sample, mem_prompt, mem_sample, mem_norm, norm_mix, norm_mem, norm_ffn, ev_w_in, ev_gate_bias, ev_w_out, ml_head_norm, mla_qa_norm, mla_kva_norm, mla_w_uq, mla_w_ukv, mla_q_norm, mla_k_norm, od_w_in, od_w_out, ret_head_norm, df_lambda, df_q_norm, df_k_norm, df_subln, xa_w_q, xa_w_kv, xa_q_norm, xa_k_norm, xa_w_o, moe_w_group, moe_w_expert, moe_w1, moe_w3, moe_w2):
    weights = (mem_norm, norm_mix, norm_mem, norm_ffn, ev_w_in, ev_gate_bias, ev_w_out, ml_head_norm,
               mla_qa_norm, mla_kva_norm, mla_w_uq, mla_w_ukv, mla_q_norm, mla_k_norm, od_w_in, od_w_out,
               ret_head_norm, df_lambda, df_q_norm, df_k_norm, df_subln, xa_w_q, xa_w_kv, xa_q_norm, xa_k_norm,
               xa_w_o, moe_w_group, moe_w_expert, moe_w1.astype(BF16), moe_w3.astype(BF16), moe_w2.astype(BF16))
    return (_trunk(x_prompt, mem_prompt, *weights), _trunk(x_sample, mem_sample, *weights))
```

```python
import functools
import math

import numpy as np
import jax
import jax.numpy as jnp
from jax import lax
from jax.experimental import pallas as pl
from jax.experimental.pallas import tpu as pltpu

F32 = jnp.float32
BF16 = jnp.bfloat16
I32 = jnp.int32

EPS = 1e-6
LOG2E = math.log2(math.e)
LANES = 128
CHUNK = 128

ML_H, ML_DH = 4, 128
ML_W = ML_H * ML_DH
MLA_H, MLA_NOPE, MLA_ROPE, MLA_V = 4, 128, 64, 128
MLA_Q_LORA, MLA_KV_LORA = 256, 128
MLA_THETA = 10000.0
MLA_DQ = MLA_NOPE + MLA_ROPE
MLA_DQP = 256
RET_H, RET_DK, RET_DV = 4, 128, 128
RET_THETA = 10000.0
RET_DECAY_EXP_FWD, RET_DECAY_EXP_BWD = 5.0, 5.5
DF_H, DF_DH = 4, 64
DF_DV = 2 * DF_DH
ROPE_THETA = 500000.0
ROPE_DIMS = DF_DH // 4
XA_H, XA_DH = 4, 128
MOE_GROUPS, MOE_PER_GROUP = 4, 8
MOE_EXPERTS = MOE_GROUPS * MOE_PER_GROUP
MOE_FF = 512
MOE_TB = 512

VMEM_LIMIT = 56 * 1024 * 1024


def _tile(n, pref):
    if n <= pref:
        return n
    t = pref - pref % 8
    while t >= 8:
        if n % t == 0:
            return t
        t -= 8
    return n


def _cparams(sem):
    return pltpu.CompilerParams(dimension_semantics=sem, vmem_limit_bytes=VMEM_LIMIT)


def _rms(x, gain):
    ms = jnp.mean(x * x, axis=-1, keepdims=True)
    return x * lax.rsqrt(ms + EPS) * gain


def _dot(a, b):
    return jnp.dot(a, b, preferred_element_type=F32)


def _dot_nt(a, b):
    return lax.dot_general(a, b, (((1,), (1,)), ((), ())), preferred_element_type=F32)


def _split_bf16(x):
    hi = x.astype(BF16)
    lo = (x - hi.astype(F32)).astype(BF16)
    return hi, lo


def _norm_matmul_body(x_ref, g_ref, w_ref, *out_refs, splits):
    h = _rms(x_ref[...], g_ref[...]).astype(BF16)
    off = 0
    for o_ref, n in zip(out_refs, splits):
        o_ref[...] = _dot(h, w_ref[:, off:off + n]).astype(o_ref.dtype)
        off += n


def norm_matmul(x2d, g, w, splits, dtypes, tm=512):
    n, d = x2d.shape
    tm = _tile(n, tm)
    ntot = w.shape[1]
    assert sum(splits) == ntot
    return pl.pallas_call(
        functools.partial(_norm_matmul_body, splits=tuple(splits)),
        grid=(n // tm,),
        in_specs=[pl.BlockSpec((tm, d), lambda i: (i, 0)),
                  pl.BlockSpec((1, d), lambda i: (0, 0)),
                  pl.BlockSpec((d, ntot), lambda i: (0, 0))],
        out_specs=[pl.BlockSpec((tm, s), lambda i: (i, 0)) for s in splits],
        out_shape=[jax.ShapeDtypeStruct((n, s), dt) for s, dt in zip(splits, dtypes)],
        compiler_params=_cparams(("parallel",)),
        name="norm_matmul",
    )(x2d, g.reshape(1, d).astype(F32), w)


def _log_sigmoid(x):
    return jnp.minimum(x, 0.0) - jnp.log1p(jnp.exp(-jnp.abs(x)))


def _mlstm_body(qf_ref, kf_ref, vf_ref, gf_ref, qb_ref, kb_ref, vb_ref, gb_ref, bias_ref,
                hf_ref, hb_ref, cn_scr, m_scr, *, nh, scale):
    L = CHUNK
    c = pl.program_id(1)

    @pl.when(c == 0)
    def _():
        cn_scr[...] = jnp.zeros_like(cn_scr)
        m_scr[...] = jnp.zeros_like(m_scr)

    row = lax.broadcasted_iota(I32, (L, L), 0)
    col = lax.broadcasted_iota(I32, (L, L), 1)
    ones_col = (lax.broadcasted_iota(I32, (L, LANES), 1) == 0).astype(BF16)
    dirs = ((qf_ref, kf_ref, vf_ref, gf_ref, hf_ref, col <= row, L - 1),
            (qb_ref, kb_ref, vb_ref, gb_ref, hb_ref, col >= row, 0))
    for d, (q_ref, k_ref, v_ref, g_ref, h_ref, mask, last) in enumerate(dirs):
        G = g_ref[0] + bias_ref[...]
        lf_hi, lf_lo = _split_bf16(_log_sigmoid(G))
        tri = mask.astype(BF16)
        Bc = _dot(tri, lf_hi) + _dot(tri, lf_lo)
        BcT = Bc.T
        GT = G.T
        for h in range(nh):
            ii = 2 * d * nh + h
            fi = (2 * d + 1) * nh + h
            ci = d * nh + h
            hs = slice(h * L, (h + 1) * L)
            q = q_ref[0, :, hs]
            k = k_ref[0, :, hs]
            v_aug = jnp.concatenate([v_ref[0, :, hs], ones_col], axis=-1)
            b_col, b_row = Bc[:, fi:fi + 1], BcT[fi:fi + 1, :]
            i_col, i_row = G[:, ii:ii + 1], GT[ii:ii + 1, :]
            b_last = Bc[last:last + 1, fi:fi + 1]
            m_prev = m_scr[ci:ci + 1, 0:1]
            cn = cn_scr[ci]

            log_d = jnp.where(mask, b_col - b_row + i_row, -jnp.inf)
            inter = b_col + m_prev
            m_t = jnp.maximum(inter, jnp.max(log_d, axis=-1, keepdims=True))
            a_inter = jnp.exp(inter - m_t)
            s = _dot_nt(q, k) * (jnp.exp(log_d - m_t) * scale)
            r = _dot(s.astype(BF16), v_aug) + a_inter * _dot(q, cn.astype(BF16))
            den = jnp.maximum(jnp.abs(r[:, L:L + 1]), jnp.exp(-m_t))
            h_ref[0, :, hs] = r[:, :L] / den

            w_log = b_last - b_col + i_col
            m_new = jnp.maximum(b_last + m_prev, jnp.max(w_log, axis=0, keepdims=True))
            dec = jnp.exp(b_last + m_prev - m_new)
            ws = jnp.exp(w_log - m_new) * scale
            kw_t = (k.astype(F32) * ws).T.astype(BF16)
            cn_scr[ci] = dec * cn + _dot(kw_t, v_aug)
            m_scr[ci:ci + 1, :] = jnp.broadcast_to(m_new, (1, LANES))


def mlstm_bidir(q, k, v, gates, bias_row):
    B, S, W = q.shape
    nh = W // CHUNK
    nc = S // CHUNK
    fwd = lambda b, c: (b, c, 0)
    bwd = lambda b, c: (b, nc - 1 - c, 0)
    blk = lambda w, im: pl.BlockSpec((1, CHUNK, w), im)
    return pl.pallas_call(
        functools.partial(_mlstm_body, nh=nh, scale=CHUNK ** -0.5),
        grid=(B, nc),
        in_specs=[blk(W, fwd), blk(W, fwd), blk(W, fwd), blk(LANES, fwd),
                  blk(W, bwd), blk(W, bwd), blk(W, bwd), blk(LANES, bwd),
                  pl.BlockSpec((1, LANES), lambda b, c: (0, 0))],
        out_specs=[blk(W, fwd), blk(W, bwd)],
        out_shape=[jax.ShapeDtypeStruct((B, S, W), F32)] * 2,
        scratch_shapes=[pltpu.VMEM((2 * nh, CHUNK, 2 * LANES), F32), pltpu.VMEM((2 * nh, LANES), F32)],
        compiler_params=_cparams(("parallel", "arbitrary")),
        name="mlstm_bidir",
    )(q, k, v, gates, q, k, v, gates, bias_row)


def _ret_body(qf_ref, kf_ref, vf_ref, cf_ref, sf_ref, qb_ref, kb_ref, cb_ref, sb_ref, vb_ref,
              mask_ref, qdec_ref, kdec_ref, yf_ref, yb_ref, st_scr, *, nh, scale, cdec):
    L = CHUNK
    c = pl.program_id(1)

    @pl.when(c == 0)
    def _():
        st_scr[...] = jnp.zeros_like(st_scr)

    def rope(x_ref, hs, cos_ref, sin_ref):
        x = x_ref[0, :, hs].astype(F32)
        return x * cos_ref[...] + pltpu.roll(x, L // 2, 1) * sin_ref[...]

    for h in range(nh):
        hs = slice(h * L, (h + 1) * L)
        q = rope(qf_ref, hs, cf_ref, sf_ref).astype(BF16)
        kf = rope(kf_ref, hs, cf_ref, sf_ref) * scale
        v = vf_ref[0, :, hs]
        st = st_scr[h]
        sc = _dot_nt(q, kf.astype(BF16)) * mask_ref[h]
        y = _dot(sc.astype(BF16), v) + _dot(q, st.astype(BF16)) * qdec_ref[h]
        yf_ref[0, :, hs] = y
        kd_t = (kf * kdec_ref[h]).T.astype(BF16)
        st_scr[h] = cdec[0][h] * st + _dot(kd_t, v)
        q = rope(qb_ref, hs, cb_ref, sb_ref).astype(BF16)
        kf = rope(kb_ref, hs, cb_ref, sb_ref) * scale
        v = vb_ref[0, :, hs]
        st = st_scr[nh + h]
        yb_ref[0, :, hs] = _dot(q, st.astype(BF16)) * qdec_ref[nh + h]
        kd_t = (kf * kdec_ref[nh + h]).T.astype(BF16)
        st_scr[nh + h] = cdec[1][h] * st + _dot(kd_t, v)


def _ret_constants():
    L = CHUNK
    heads = np.arange(RET_H, dtype=np.float64)
    lg_f = np.log1p(-np.exp2(-RET_DECAY_EXP_FWD - heads))
    lg_b = np.log1p(-np.exp2(-RET_DECAY_EXP_BWD - heads))
    pos = np.arange(L, dtype=np.float64)
    dist = pos[:, None] - pos[None, :]
    mask = np.where(dist[None] >= 0, np.exp(np.abs(dist)[None] * lg_f[:, None, None]),
                    np.exp(np.abs(dist)[None] * lg_b[:, None, None]))
    qdec = np.concatenate([np.exp((pos + 1.0)[None] * lg_f[:, None]), np.exp((L - pos)[None] * lg_b[:, None])])
    kdec = np.concatenate([np.exp((L - 1.0 - pos)[None] * lg_f[:, None]), np.exp(pos[None] * lg_b[:, None])])
    cdec = (tuple(float(x) for x in np.exp(L * lg_f)), tuple(float(x) for x in np.exp(L * lg_b)))
    as32 = lambda a: jnp.asarray(a, F32)
    return as32(mask), as32(qdec[:, :, None]), as32(kdec[:, :, None]), cdec


def retention_bidir(q, k, v, cos_t, sin_t):
    B, S, W = q.shape
    nh = W // CHUNK
    nc = S // CHUNK
    mask, qdec, kdec, cdec = _ret_constants()
    fwd = lambda b, c: (b, c, 0)
    bwd = lambda b, c: (b, nc - 1 - c, 0)
    blk = lambda im: pl.BlockSpec((1, CHUNK, W), im)
    tab_f = pl.BlockSpec((CHUNK, LANES), lambda b, c: (c, 0))
    tab_b = pl.BlockSpec((CHUNK, LANES), lambda b, c: (nc - 1 - c, 0))
    const = lambda shp: pl.BlockSpec(shp, lambda b, c: (0,) * len(shp))
    return pl.pallas_call(
        functools.partial(_ret_body, nh=nh, scale=RET_DK ** -0.5, cdec=cdec),
        grid=(B, nc),
        in_specs=[blk(fwd), blk(fwd), blk(fwd), tab_f, tab_f, blk(bwd), blk(bwd), tab_b, tab_b, blk(bwd),
                  const(mask.shape), const(qdec.shape), const(kdec.shape)],
        out_specs=[blk(fwd), blk(bwd)],
        out_shape=[jax.ShapeDtypeStruct((B, S, W), F32)] * 2,
        scratch_shapes=[pltpu.VMEM((2 * nh, CHUNK, CHUNK), F32)],
        compiler_params=_cparams(("parallel", "arbitrary")),
        name="retention_bidir",
    )(q, k, v, cos_t, sin_t, q, k, cos_t, sin_t, v, mask, qdec, kdec)


def _flash_body(q_ref, k_ref, v_ref, o_ref, s0, s1, x0, x1, p0, p1, a0, a1, m_scr, acc_scr, *, dv, tkb):
    q = q_ref[0]
    nkb = k_ref.shape[1] // tkb
    nl = tkb // LANES
    ones_col = (lax.broadcasted_iota(I32, (tkb, LANES), 1) == 0).astype(BF16)
    lane_tile = lambda c: slice(c * LANES, (c + 1) * LANES)

    def scores(n, s_buf, x_buf):
        s = _dot_nt(q, k_ref[0, pl.ds(pl.multiple_of(n * tkb, tkb), tkb), :])
        s_buf[...] = s
        mx = s[:, lane_tile(0)]
        for c in range(1, nl):
            mx = jnp.maximum(mx, s[:, lane_tile(c)])
        x_buf[...] = mx

    def softmax(s_buf, x_buf, p_buf, a_buf):
        m_prev = m_scr[...]
        m_new = jnp.maximum(m_prev, jnp.max(x_buf[...], axis=-1, keepdims=True))
        a_buf[...] = jnp.exp2(m_prev - m_new)
        m_scr[...] = m_new
        for c in range(nl):
            p_buf[:, lane_tile(c)] = jnp.exp2(s_buf[:, lane_tile(c)] - m_new).astype(BF16)

    def weighted_sum(n, p_buf, a_buf):
        v_aug = jnp.concatenate([v_ref[0, pl.ds(pl.multiple_of(n * tkb, tkb), tkb), :], ones_col], axis=-1)
        alpha = a_buf[...]
        alpha = jnp.concatenate([alpha] * (acc_scr.shape[1] // LANES), axis=-1)
        acc_scr[...] = alpha * acc_scr[...] + _dot(p_buf[...], v_aug)

    m_scr[...] = jnp.full_like(m_scr, -jnp.inf)
    acc_scr[...] = jnp.zeros_like(acc_scr)
    scores(0, s0, x0)
    scores(1, s1, x1)
    softmax(s0, x0, p0, a0)

    def pair(i, carry):
        n = 2 * i + 1
        scores(n + 1, s0, x0)
        softmax(s1, x1, p1, a1)
        weighted_sum(n - 1, p0, a0)
        scores(n + 2, s1, x1)
        softmax(s0, x0, p0, a0)
        weighted_sum(n, p1, a1)
        return carry

    lax.fori_loop(0, (nkb - 2) // 2, pair, 0)
    softmax(s1, x1, p1, a1)
    weighted_sum(nkb - 2, p0, a0)
    weighted_sum(nkb - 1, p1, a1)
    acc = acc_scr[...]
    o_ref[0] = (acc[:, :dv] / acc[:, dv:dv + 1]).astype(o_ref.dtype)


def flash_attention(q, k, v, nheads, kv_group, dq, dv, tq=512, tkb=1024):
    B, S, _ = q.shape
    T = k.shape[1]
    tq = _tile(S, tq)
    tkb = min(tkb, T // 2)
    assert T % (2 * tkb) == 0 and tkb % LANES == 0, "key length must split into an even number of blocks"
    vm = pltpu.VMEM
    return pl.pallas_call(
        functools.partial(_flash_body, dv=dv, tkb=tkb),
        grid=(B, nheads, S // tq),
        in_specs=[pl.BlockSpec((1, tq, dq), lambda b, h, i: (b, i, h)),
                  pl.BlockSpec((1, T, dq), lambda b, h, i: (b, 0, h // kv_group)),
                  pl.BlockSpec((1, T, dv), lambda b, h, i: (b, 0, h // kv_group))],
        out_specs=pl.BlockSpec((1, tq, dv), lambda b, h, i: (b, i, h)),
        out_shape=jax.ShapeDtypeStruct((B, S, nheads * dv), BF16),
        scratch_shapes=[vm((tq, tkb), F32), vm((tq, tkb), F32), vm((tq, LANES), F32), vm((tq, LANES), F32),
                        vm((tq, tkb), BF16), vm((tq, tkb), BF16), vm((tq, LANES), F32), vm((tq, LANES), F32),
                        vm((tq, LANES), F32), vm((tq, dv + LANES), F32)],
        compiler_params=_cparams(("parallel", "parallel", "parallel")),
        name="flash_attention",
    )(q, k, v)


def _mla_prep_body(cq_ref, ckv_ref, kpe_ref, cos_ref, sin_ref, qan_ref, kvan_ref, wuq_ref, wuk_ref, wuv_ref,
                   qn_ref, kn_ref, q_out, k_out, v_out, *, scale):
    cos, sin = cos_ref[...], sin_ref[...]

    def rot(x):
        return x * cos + pltpu.roll(x, LANES // 2, 1) * sin

    cqn = _rms(cq_ref[0].astype(F32), qan_ref[...]).astype(BF16)
    uq = _dot(cqn, wuq_ref[...])
    ckvn = _rms(ckv_ref[0].astype(F32), kvan_ref[...]).astype(BF16)
    uk = _dot(ckvn, wuk_ref[...])
    v_out[0] = _dot(ckvn, wuv_ref[...]).astype(v_out.dtype)
    kpe = kpe_ref[0].astype(F32)
    kpe_ss = jnp.sum(kpe * kpe, axis=-1, keepdims=True)
    qn, kn = qn_ref[...], kn_ref[...]
    for h in range(MLA_H):
        lo = h * MLA_DQP
        blk = uq[:, lo:lo + MLA_DQP]
        r = lax.rsqrt(jnp.sum(blk * blk, axis=-1, keepdims=True) * (1.0 / MLA_DQ) + EPS) * scale
        y = blk * r * qn
        q_out[0, :, lo:lo + LANES] = y[:, :LANES].astype(q_out.dtype)
        q_out[0, :, lo + LANES:lo + MLA_DQP] = rot(y[:, LANES:]).astype(q_out.dtype)
        kh = uk[:, h * LANES:(h + 1) * LANES]
        r = lax.rsqrt((jnp.sum(kh * kh, axis=-1, keepdims=True) + kpe_ss) * (1.0 / MLA_DQ) + EPS)
        k_out[0, :, lo:lo + LANES] = (kh * r * kn[:, :LANES]).astype(k_out.dtype)
        k_out[0, :, lo + LANES:lo + MLA_DQP] = rot(kpe * r * kn[:, LANES:]).astype(k_out.dtype)


def _spread_rope64(a, axis):
    x1, x2 = jnp.split(a, 2, axis=axis)
    z = jnp.zeros_like(x1)
    return jnp.concatenate([x1, z, x2, z], axis=axis)


def mla_prep(cq, ckv, kpe, w_uq, w_ukv, qa_n, kva_n, q_n, k_n, tm=512):
    B, S, _ = cq.shape
    tm = _tile(S, tm)
    half = MLA_ROPE // 2
    inv = 1.0 / (MLA_THETA ** (jnp.arange(0, MLA_ROPE, 2, dtype=F32) / MLA_ROPE))
    ang = jnp.arange(S, dtype=F32)[:, None] * inv[None, :]
    c, s = jnp.cos(ang), jnp.sin(ang)
    z = jnp.zeros((S, half), F32)
    cos_t = jnp.concatenate([c, z, c, z], axis=1)
    sin_t = jnp.concatenate([-s, z, s, z], axis=1)
    wq = w_uq.reshape(MLA_Q_LORA, MLA_H, MLA_DQ)
    wq = jnp.concatenate([wq[..., :MLA_NOPE], _spread_rope64(wq[..., MLA_NOPE:], 2)], axis=2)
    wq = wq.reshape(MLA_Q_LORA, MLA_H * MLA_DQP).astype(BF16)
    wkv = w_ukv.reshape(MLA_KV_LORA, MLA_H, MLA_NOPE + MLA_V)
    wuk = wkv[..., :MLA_NOPE].reshape(MLA_KV_LORA, MLA_H * MLA_NOPE).astype(BF16)
    wuv = wkv[..., MLA_NOPE:].reshape(MLA_KV_LORA, MLA_H * MLA_V).astype(BF16)
    pad_gain = lambda g: jnp.concatenate([g[:MLA_NOPE], _spread_rope64(g[MLA_NOPE:], 0)]).reshape(1, MLA_DQP).astype(F32)
    tok = lambda w: pl.BlockSpec((1, tm, w), lambda b, i: (b, i, 0))
    tab = pl.BlockSpec((tm, LANES), lambda b, i: (i, 0))
    const = lambda a: pl.BlockSpec(a.shape, lambda b, i: (0, 0))
    args = [cq, ckv, kpe, cos_t, sin_t, qa_n.reshape(1, -1).astype(F32), kva_n.reshape(1, -1).astype(F32),
            wq, wuk, wuv, pad_gain(q_n), pad_gain(k_n)]
    return pl.pallas_call(
        functools.partial(_mla_prep_body, scale=MLA_DQ ** -0.5 * LOG2E),
        grid=(B, S // tm),
        in_specs=[tok(MLA_Q_LORA), tok(MLA_KV_LORA), tok(LANES), tab, tab] + [const(a) for a in args[5:]],
        out_specs=[tok(MLA_H * MLA_DQP), tok(MLA_H * MLA_DQP), tok(MLA_H * MLA_V)],
        out_shape=[jax.ShapeDtypeStruct((B, S, MLA_H * MLA_DQP), BF16),
                   jax.ShapeDtypeStruct((B, S, MLA_H * MLA_DQP), BF16),
                   jax.ShapeDtypeStruct((B, S, MLA_H * MLA_V), BF16)],
        compiler_params=_cparams(("parallel", "parallel")),
        name="mla_prep",
    )(*args)


def _diff_prep_body(dq_ref, dk_ref, cos_ref, s1_ref, s2_ref, qn_ref, kn_ref, seg_ref, q_out, k_out, *, scale):
    cos, s1, s2 = cos_ref[...], s1_ref[...], s2_ref[...]
    lane = lax.broadcasted_iota(I32, (1, LANES), 1)

    def norm_rot(x_ref, gain):
        x = x_ref[0].astype(F32)
        hi, lo = _split_bf16(x * x)
        ms = (_dot(hi, seg_ref[...]) + _dot(lo, seg_ref[...])) * (1.0 / DF_DH)
        y = x * lax.rsqrt(ms + EPS) * gain
        out = []
        for b in range(DF_H):
            yb = y[:, b * LANES:(b + 1) * LANES]
            out.append(yb * cos + pltpu.roll(yb, LANES - ROPE_DIMS // 2, 1) * s1
                       + pltpu.roll(yb, ROPE_DIMS // 2, 1) * s2)
        return out

    qs = norm_rot(dq_ref, qn_ref[...])
    ks = norm_rot(dk_ref, kn_ref[...])
    for b in range(DF_H):
        qb = qs[b] * scale
        q_out[0, :, (2 * b) * LANES:(2 * b + 1) * LANES] = jnp.where(lane < DF_DH, qb, 0.0).astype(q_out.dtype)
        q_out[0, :, (2 * b + 1) * LANES:(2 * b + 2) * LANES] = jnp.where(lane >= DF_DH, qb, 0.0).astype(q_out.dtype)
        k_out[0, :, b * LANES:(b + 1) * LANES] = ks[b].astype(k_out.dtype)


def diff_prep(dq, dk, q_n, k_n, tm=512):
    B, S, W = dq.shape
    tm = _tile(S, tm)
    half = ROPE_DIMS // 2
    inv = 1.0 / (ROPE_THETA ** (jnp.arange(0, ROPE_DIMS, 2, dtype=F32) / ROPE_DIMS))
    ang = jnp.arange(S, dtype=F32)[:, None] * inv[None, :]
    c, s = jnp.cos(ang), jnp.sin(ang)
    rest = DF_DH - ROPE_DIMS
    seg_c = jnp.concatenate([c, c, jnp.ones((S, rest), F32)], axis=1)
    seg_s1 = jnp.concatenate([-s, jnp.zeros((S, half + rest), F32)], axis=1)
    seg_s2 = jnp.concatenate([jnp.zeros((S, half), F32), s, jnp.zeros((S, rest), F32)], axis=1)
    two = lambda a: jnp.concatenate([a, a], axis=1)
    seg = jnp.asarray(np.kron(np.eye(W // DF_DH), np.ones((DF_DH, DF_DH))), BF16)
    gain = lambda g: jnp.tile(g.astype(F32), W // DF_DH).reshape(1, W)
    tok = lambda w: pl.BlockSpec((1, tm, w), lambda b, i: (b, i, 0))
    tab = pl.BlockSpec((tm, LANES), lambda b, i: (i, 0))
    const = lambda shp: pl.BlockSpec(shp, lambda b, i: (0, 0))
    return pl.pallas_call(
        functools.partial(_diff_prep_body, scale=DF_DH ** -0.5 * LOG2E),
        grid=(B, S // tm),
        in_specs=[tok(W), tok(W), tab, tab, tab, const((1, W)), const((1, W)), const((W, W))],
        out_specs=[tok(2 * W), tok(W)],
        out_shape=[jax.ShapeDtypeStruct((B, S, 2 * W), BF16), jax.ShapeDtypeStruct((B, S, W), BF16)],
        compiler_params=_cparams(("parallel", "parallel")),
        name="diff_prep",
    )(dq, dk, two(seg_c), two(seg_s1), two(seg_s2), gain(q_n), gain(k_n), seg)


def _head_rms(y, gain, nh):
    out = []
    for h in range(nh):
        hs = slice(h * LANES, (h + 1) * LANES)
        out.append(_rms(y[:, hs], gain[:, hs]))
    return jnp.concatenate(out, axis=-1)


def _even_out_body(x_ref, hf_ref, hb_ref, mo_ref, oa_ref, hn_ref, w_ref, o_ref):
    hm = _head_rms(hf_ref[...] + hb_ref[...], hn_ref[...], ML_H)
    hm = (hm * jax.nn.sigmoid(mo_ref[...].astype(F32))).astype(BF16)
    o_ref[...] = x_ref[...] + _dot(hm, w_ref[:ML_W, :]) + _dot(oa_ref[...], w_ref[ML_W:, :])


def _odd_out_body(x_ref, yf_ref, yb_ref, rg_ref, od_ref, hn_ref, sub_ref, lam_ref, w_ref, o_ref, *, lam_init):
    yr = _head_rms(yf_ref[...] + yb_ref[...], hn_ref[...], RET_H)
    yr = (yr * jax.nn.silu(rg_ref[...].astype(F32))).astype(BF16)
    lp = lam_ref[...]
    lam = (jnp.exp(jnp.sum(lp[0:1] * lp[1:2], axis=-1, keepdims=True))
           - jnp.exp(jnp.sum(lp[2:3] * lp[3:4], axis=-1, keepdims=True)) + lam_init)
    od = od_ref[...].astype(F32)
    outs = []
    for h in range(DF_H):
        o1 = od[:, (2 * h) * LANES:(2 * h + 1) * LANES]
        o2 = od[:, (2 * h + 1) * LANES:(2 * h + 2) * LANES]
        outs.append(_rms(o1 - lam * o2, sub_ref[...]) * (1.0 - lam_init))
    o = jnp.concatenate(outs, axis=-1).astype(BF16)
    nr = RET_H * RET_DV
    o_ref[...] = x_ref[...] + _dot(yr, w_ref[:nr, :]) + _dot(o, w_ref[nr:, :])


def _rowwise_call(body, x2d, row_args, const_args, name, tm=512):
    n, d = x2d.shape
    tm = _tile(n, tm)
    rows = [x2d] + list(row_args)
    return pl.pallas_call(
        body,
        grid=(n // tm,),
        in_specs=[pl.BlockSpec((tm, a.shape[1]), lambda i: (i, 0)) for a in rows]
        + [pl.BlockSpec(a.shape, lambda i: (0, 0)) for a in const_args],
        out_specs=pl.BlockSpec((tm, d), lambda i: (i, 0)),
        out_shape=jax.ShapeDtypeStruct((n, d), F32),
        compiler_params=_cparams(("parallel",)),
        name=name,
    )(*rows, *const_args)


def _xattn_body(x_ref, k_ref, v_ref, g_ref, wq_ref, qn_ref, kn_ref, wo_ref, o_ref, kn_scr, *, scale):
    @pl.when(pl.program_id(1) == 0)
    def _():
        kk = k_ref[0].astype(F32)
        for h in range(XA_H):
            hs = slice(h * XA_DH, (h + 1) * XA_DH)
            kn_scr[:, hs] = _rms(kk[:, hs], kn_ref[...]).astype(BF16)

    x = x_ref[0]
    q = _dot(_rms(x, g_ref[...]).astype(BF16), wq_ref[...])
    outs = []
    for h in range(XA_H):
        hs = slice(h * XA_DH, (h + 1) * XA_DH)
        qh = (_rms(q[:, hs], qn_ref[...]) * scale).astype(BF16)
        s = _dot_nt(qh, kn_scr[:, hs])
        p = jnp.exp(s - jnp.max(s, axis=-1, keepdims=True))
        o = _dot(p.astype(BF16), v_ref[0, :, hs]) / jnp.sum(p, axis=-1, keepdims=True)
        outs.append(o.astype(BF16))
    o_ref[0] = x + _dot(jnp.concatenate(outs, axis=-1), wo_ref[...])


def memory_xattn(x, k, v, g, wq, qn, kn, wo, tm=512):
    B, S, D = x.shape
    M = k.shape[1]
    W = XA_H * XA_DH
    tm = _tile(S, tm)
    const = lambda a: pl.BlockSpec(a.shape, lambda b, i: (0, 0))
    consts = [g.reshape(1, D).astype(F32), wq, qn.reshape(1, XA_DH).astype(F32), kn.reshape(1, XA_DH).astype(F32), wo]
    return pl.pallas_call(
        functools.partial(_xattn_body, scale=XA_DH ** -0.5),
        grid=(B, S // tm),
        in_specs=[pl.BlockSpec((1, tm, D), lambda b, i: (b, i, 0)),
                  pl.BlockSpec((1, M, W), lambda b, i: (b, 0, 0)),
                  pl.BlockSpec((1, M, W), lambda b, i: (b, 0, 0))] + [const(a) for a in consts],
        out_specs=pl.BlockSpec((1, tm, D), lambda b, i: (b, i, 0)),
        out_shape=jax.ShapeDtypeStruct((B, S, D), F32),
        scratch_shapes=[pltpu.VMEM((M, W), BF16)],
        compiler_params=_cparams(("parallel", "arbitrary")),
        name="memory_xattn",
    )(x, k, v, *consts)


PK_E1, PK_E2, PK_R1, PK_R2, PK_W1, PK_W2 = range(6)


def _router_body(x_ref, g_ref, whi_ref, wlo_ref, h_ref, pk_ref, cnt_ref, carry_scr):
    i = pl.program_id(0)

    @pl.when(i == 0)
    def _():
        carry_scr[...] = jnp.zeros_like(carry_scr)

    h = _rms(x_ref[...], g_ref[...])
    h_ref[...] = h
    tm = h.shape[0]
    h_hi, h_lo = _split_bf16(h)
    logits = _dot(h_hi, whi_ref[...]) + (_dot(h_lo, whi_ref[...]) + _dot(h_hi, wlo_ref[...]))
    lane = lax.broadcasted_iota(I32, (tm, LANES), 1)
    neg = -jnp.inf
    big = 4 * LANES

    def top(vals):
        v = jnp.max(vals, axis=-1, keepdims=True)
        idx = jnp.min(jnp.where(vals == v, lane, big), axis=-1, keepdims=True)
        return v, idx

    gl = jnp.where((lane >= MOE_EXPERTS) & (lane < MOE_EXPERTS + MOE_GROUPS), logits, neg)
    gmax, glane = top(gl)
    p_grp = 1.0 / jnp.sum(jnp.exp(gl - gmax), axis=-1, keepdims=True)
    gidx = glane - MOE_EXPERTS
    grp_of_lane = lax.shift_right_logical(lane, int(math.log2(MOE_PER_GROUP)))
    el = jnp.where((lane < MOE_EXPERTS) & (grp_of_lane == gidx), logits, neg)
    v1, i1 = top(el)
    v2, i2 = top(jnp.where(lane == i1, neg, el))
    e21 = jnp.exp(v2 - v1)
    w1 = p_grp / (1.0 + e21)
    w2 = w1 * e21
    twohot = ((lane == i1) | (lane == i2)).astype(F32)
    r = lax.broadcasted_iota(I32, (tm, tm), 0)
    c = lax.broadcasted_iota(I32, (tm, tm), 1)
    before = _dot((c < r).astype(BF16), twohot.astype(BF16)) + carry_scr[...]
    r1 = jnp.sum(jnp.where(lane == i1, before, 0.0), axis=-1, keepdims=True)
    r2 = jnp.sum(jnp.where(lane == i2, before, 0.0), axis=-1, keepdims=True)
    carry_scr[...] += jnp.sum(twohot, axis=0, keepdims=True)
    cnt_ref[...] = carry_scr[...]
    pk = jnp.zeros((tm, LANES), F32)
    for ln, val in ((PK_E1, i1.astype(F32)), (PK_E2, i2.astype(F32)), (PK_R1, r1), (PK_R2, r2),
                    (PK_W1, w1), (PK_W2, w2)):
        pk = jnp.where(lane == ln, val, pk)
    pk_ref[...] = pk


def moe_router(x2d, g, wg, we, tm=512):
    n, d = x2d.shape
    tm = _tile(n, tm)
    wr = jnp.concatenate([we, wg, jnp.zeros((d, LANES - MOE_EXPERTS - MOE_GROUPS), F32)], axis=1)
    w_hi, w_lo = _split_bf16(wr)
    return pl.pallas_call(
        _router_body,
        grid=(n // tm,),
        in_specs=[pl.BlockSpec((tm, d), lambda i: (i, 0)), pl.BlockSpec((1, d), lambda i: (0, 0)),
                  pl.BlockSpec((d, LANES), lambda i: (0, 0)), pl.BlockSpec((d, LANES), lambda i: (0, 0))],
        out_specs=[pl.BlockSpec((tm, d), lambda i: (i, 0)), pl.BlockSpec((tm, LANES), lambda i: (i, 0)),
                   pl.BlockSpec((1, LANES), lambda i: (0, 0))],
        out_shape=[jax.ShapeDtypeStruct((n, d), F32), jax.ShapeDtypeStruct((n, LANES), F32),
                   jax.ShapeDtypeStruct((1, LANES), F32)],
        scratch_shapes=[pltpu.VMEM((1, LANES), F32)],
        compiler_params=_cparams(("arbitrary",)),
        name="moe_router",
    )(x2d, g.reshape(1, d).astype(F32), w_hi, w_lo)


def _slots_body(pk_ref, ps_ref, o_ref):
    pk = pk_ref[...]
    lane = lax.broadcasted_iota(I32, pk.shape, 1)
    lane_f = lane.astype(F32)
    start = lambda e: jnp.sum(jnp.where(lane_f == e, ps_ref[...], 0.0), axis=-1, keepdims=True)
    d1 = start(pk[:, PK_E1:PK_E1 + 1]) + pk[:, PK_R1:PK_R1 + 1]
    d2 = start(pk[:, PK_E2:PK_E2 + 1]) + pk[:, PK_R2:PK_R2 + 1]
    o_ref[...] = jnp.where(lane == 0, d1, jnp.where(lane == 1, d2, 0.0)).astype(I32)


def moe_slots(pk, pad_start_row, tm=1024):
    n = pk.shape[0]
    tm = _tile(n, tm)
    return pl.pallas_call(
        _slots_body,
        grid=(n // tm,),
        in_specs=[pl.BlockSpec((tm, LANES), lambda i: (i, 0)), pl.BlockSpec((1, LANES), lambda i: (0, 0))],
        out_specs=pl.BlockSpec((tm, LANES), lambda i: (i, 0)),
        out_shape=jax.ShapeDtypeStruct((n, LANES), I32),
        compiler_params=_cparams(("parallel",)),
        name="moe_slots",
    )(pk, pad_start_row)


ROW_UNROLL = 8


def _for_each_row(tt, fn):
    def group(u, carry):
        for j in range(ROW_UNROLL):
            fn(u * ROW_UNROLL + j)
        return carry
    lax.fori_loop(0, tt // ROW_UNROLL, group, 0)


def _dispatch_body(idx_ref, h_ref, xb_in, xb_hbm, sem, *, tt):
    del xb_in

    def copies(t):
        src = h_ref.at[pl.ds(t, 1)]
        return (pltpu.make_async_copy(src, xb_hbm.at[pl.ds(idx_ref[0, 0, 2 * t], 1)], sem),
                pltpu.make_async_copy(src, xb_hbm.at[pl.ds(idx_ref[0, 0, 2 * t + 1], 1)], sem))

    def issue(t):
        for c in copies(t):
            c.start()

    def drain(t):
        for c in copies(t):
            c.wait()

    _for_each_row(tt, issue)
    _for_each_row(tt, drain)


def moe_dispatch(h2d, idx, nslots, tt):
    n, d = h2d.shape
    xb0 = jnp.zeros((nslots, d), F32)
    return pl.pallas_call(
        functools.partial(_dispatch_body, tt=tt),
        grid=(n // tt,),
        in_specs=[pl.BlockSpec((1, 1, 2 * tt), lambda i: (i, 0, 0), memory_space=pltpu.SMEM),
                  pl.BlockSpec((tt, d), lambda i: (i, 0)),
                  pl.BlockSpec(memory_space=pl.ANY)],
        out_specs=pl.BlockSpec(memory_space=pl.ANY),
        scratch_shapes=[pltpu.SemaphoreType.DMA(())],
        out_shape=jax.ShapeDtypeStruct((nslots, d), F32),
        input_output_aliases={2: 0},
        compiler_params=pltpu.CompilerParams(dimension_semantics=("arbitrary",), has_side_effects=True,
                                             vmem_limit_bytes=VMEM_LIMIT),
        name="moe_dispatch",
    )(idx, h2d, xb0)


def _expert_body(be_ref, nused_ref, xb_ref, w1_ref, w3_ref, w2_ref, yb_ref):
    i = pl.program_id(0)

    @pl.when(i < nused_ref[0])
    def _():
        x = xb_ref[...].astype(BF16)
        a = _dot(x, w1_ref[0])
        b = _dot(x, w3_ref[0])
        yb_ref[...] = _dot((jax.nn.silu(a) * b).astype(BF16), w2_ref[0])

    @pl.when(i >= nused_ref[0])
    def _():
        yb_ref[...] = jnp.zeros_like(yb_ref)


def moe_experts(xb, blk_e, nused, w1, w3, w2):
    p, d = xb.shape
    ff = w1.shape[2]
    return pl.pallas_call(
        _expert_body,
        grid_spec=pltpu.PrefetchScalarGridSpec(
            num_scalar_prefetch=2,
            grid=(p // MOE_TB,),
            in_specs=[pl.BlockSpec((MOE_TB, d), lambda i, be, nu: (i, 0)),
                      pl.BlockSpec((1, d, ff), lambda i, be, nu: (be[i], 0, 0)),
                      pl.BlockSpec((1, d, ff), lambda i, be, nu: (be[i], 0, 0)),
                      pl.BlockSpec((1, ff, d), lambda i, be, nu: (be[i], 0, 0))],
            out_specs=pl.BlockSpec((MOE_TB, d), lambda i, be, nu: (i, 0))),
        out_shape=jax.ShapeDtypeStruct((p, d), F32),
        compiler_params=_cparams(("arbitrary",)),
        name="moe_experts",
    )(blk_e, nused, xb, w1, w3, w2)


def _combine_body(idx_ref, x_ref, pk_ref, yb_hbm, o_ref, y1_scr, y2_scr, sem, *, tt):
    def copies(t):
        return (pltpu.make_async_copy(yb_hbm.at[pl.ds(idx_ref[0, 0, 2 * t], 1)], y1_scr.at[pl.ds(t, 1)], sem),
                pltpu.make_async_copy(yb_hbm.at[pl.ds(idx_ref[0, 0, 2 * t + 1], 1)], y2_scr.at[pl.ds(t, 1)], sem))

    def issue(t):
        for c in copies(t):
            c.start()

    def drain(t):
        for c in copies(t):
            c.wait()

    _for_each_row(tt, issue)
    _for_each_row(tt, drain)
    pk = pk_ref[...]
    o_ref[...] = (x_ref[...] + pk[:, PK_W1:PK_W1 + 1] * y1_scr[...]) + pk[:, PK_W2:PK_W2 + 1] * y2_scr[...]


def moe_combine(x2d, pk, idx, yb, tt):
    n, d = x2d.shape
    return pl.pallas_call(
        functools.partial(_combine_body, tt=tt),
        grid=(n // tt,),
        in_specs=[pl.BlockSpec((1, 1, 2 * tt), lambda i: (i, 0, 0), memory_space=pltpu.SMEM),
                  pl.BlockSpec((tt, d), lambda i: (i, 0)),
                  pl.BlockSpec((tt, LANES), lambda i: (i, 0)),
                  pl.BlockSpec(memory_space=pl.ANY)],
        out_specs=pl.BlockSpec((tt, d), lambda i: (i, 0)),
        scratch_shapes=[pltpu.VMEM((tt, d), F32), pltpu.VMEM((tt, d), F32), pltpu.SemaphoreType.DMA(())],
        out_shape=jax.ShapeDtypeStruct((n, d), F32),
        compiler_params=_cparams(("arbitrary",)),
        name="moe_combine",
    )(idx, x2d, pk, yb)


def hier_moe_residual(x2d, g, wg, we, w1, w3, w2):
    n, d = x2d.shape
    h, pk, cnt = moe_router(x2d, g, wg, we)
    tt = _tile(n, 512)
    counts = cnt[0, :MOE_EXPERTS].astype(I32)
    padded = ((counts + MOE_TB - 1) // MOE_TB) * MOE_TB
    pad_end = jnp.cumsum(padded)
    pad_start_row = jnp.zeros((1, LANES), F32).at[0, :MOE_EXPERTS].set((pad_end - padded).astype(F32))
    idx = moe_slots(pk, pad_start_row)[:, :2].reshape(n // tt, 1, 2 * tt)
    nslots = 2 * n + MOE_EXPERTS * MOE_TB
    nblk = nslots // MOE_TB
    blk_e = jnp.sum(pad_end[None, :] <= (jnp.arange(nblk, dtype=I32) * MOE_TB)[:, None], axis=1)
    last_e = jnp.max(jnp.where(counts > 0, jnp.arange(MOE_EXPERTS, dtype=I32), 0))
    blk_e = jnp.minimum(blk_e, last_e).astype(I32)
    nused = (pad_end[-1:] // MOE_TB).astype(I32)
    xb = moe_dispatch(h, idx, nslots, tt)
    yb = moe_experts(xb, blk_e, nused, w1, w3, w2)
    return moe_combine(x2d, pk, idx, yb, tt)


def _even_layer(x, norm_g, w_in, gate_bias, w_out, ml_hn, qa_n, kva_n, w_uq, w_ukv, q_n, k_n):
    B, S, D = x.shape
    n = B * S
    ng = 4 * ML_H
    cols = np.cumsum([0, ML_W, ML_W, ML_W, ML_W, ng, MLA_Q_LORA, MLA_KV_LORA, MLA_ROPE])
    part = lambda j: w_in[:, cols[j]:cols[j + 1]]
    w_g = jnp.concatenate([part(4), jnp.zeros((D, LANES - ng), F32)], axis=1)
    w_cat = jnp.concatenate([part(0), part(1), part(2), part(3), part(5), part(6), _spread_rope64(part(7), 1), w_g],
                            axis=1).astype(BF16)
    splits = [ML_W, ML_W, ML_W, ML_W, MLA_Q_LORA, MLA_KV_LORA, LANES, LANES]
    mq, mk, mv, mo, cq, ckv, kpe, mg = norm_matmul(x.reshape(n, D), norm_g, w_cat, splits, [BF16] * 7 + [F32])
    r3 = lambda a: a.reshape(B, S, a.shape[-1])
    bias_row = jnp.concatenate([gate_bias.reshape(ng).astype(F32), jnp.zeros((LANES - ng,), F32)]).reshape(1, LANES)
    hf, hb = mlstm_bidir(r3(mq), r3(mk), r3(mv), r3(mg), bias_row)
    qf, kf, vv = mla_prep(r3(cq), r3(ckv), r3(kpe), w_uq, w_ukv, qa_n, kva_n, q_n, k_n)
    o_mla = flash_attention(qf, kf, vv, MLA_H, 1, MLA_DQP, MLA_V)
    out = _rowwise_call(_even_out_body, x.reshape(n, D),
                        [hf.reshape(n, ML_W), hb.reshape(n, ML_W), mo, o_mla.reshape(n, MLA_H * MLA_V)],
                        [ml_hn.reshape(1, ML_W).astype(F32), w_out.astype(BF16)], "even_out")
    return out.reshape(B, S, D)


def _odd_layer(x, layer_idx, norm_g, w_in, w_out, ret_hn, lam_p, q_n, k_n, subln):
    B, S, D = x.shape
    n = B * S
    W = RET_H * RET_DK
    rq, rk, rv, rg, dq, dk, dv = norm_matmul(x.reshape(n, D), norm_g, w_in.astype(BF16), [W] * 7, [BF16] * 7)
    r3 = lambda a: a.reshape(B, S, a.shape[-1])
    inv = 1.0 / (RET_THETA ** (jnp.arange(0, RET_DK, 2, dtype=F32) / RET_DK))
    ang = jnp.arange(S, dtype=F32)[:, None] * inv[None, :]
    c, s = jnp.cos(ang), jnp.sin(ang)
    yf, yb = retention_bidir(r3(rq), r3(rk), r3(rv), jnp.concatenate([c, c], axis=1), jnp.concatenate([-s, s], axis=1))
    qd, kd = diff_prep(r3(dq), r3(dk), q_n, k_n)
    od = flash_attention(qd, kd, r3(dv), 2 * DF_H, 2, LANES, DF_DV)
    lam_init = 0.8 - 0.6 * math.exp(-0.3 * layer_idx)
    lam_rows = jnp.concatenate([lam_p.astype(F32), jnp.zeros((4, LANES - DF_DH), F32)], axis=1)
    out = _rowwise_call(functools.partial(_odd_out_body, lam_init=lam_init), x.reshape(n, D),
                        [yf.reshape(n, W), yb.reshape(n, W), rg, od.reshape(n, 2 * W)],
                        [ret_hn.reshape(1, W).astype(F32), subln.reshape(1, DF_DV).astype(F32), lam_rows,
                         w_out.astype(BF16)], "odd_out")
    return out.reshape(B, S, D)


def _trunk(x, mem, mem_norm, norm_mix, norm_mem, norm_ffn, ev_w_in, ev_gate_bias, ev_w_out, ml_head_norm,
           mla_qa_norm, mla_kva_norm, mla_w_uq, mla_w_ukv, mla_q_norm, mla_k_norm, od_w_in, od_w_out,
           ret_head_norm, df_lambda, df_q_norm, df_k_norm, df_subln, xa_w_q, xa_w_kv, xa_q_norm, xa_k_norm,
           xa_w_o, moe_w_group, moe_w_expert, moe_w1, moe_w3, moe_w2):
    B, S, D = x.shape
    depth = norm_mix.shape[0]
    M = mem.shape[1]
    W = XA_H * XA_DH
    for layer in range(depth):
        j = layer // 2
        if layer % 2 == 0:
            x = _even_layer(x, norm_mix[layer], ev_w_in[j], ev_gate_bias[j], ev_w_out[j], ml_head_norm[j],
                            mla_qa_norm[j], mla_kva_norm[j], mla_w_uq[j], mla_w_ukv[j], mla_q_norm[j], mla_k_norm[j])
        else:
            x = _odd_layer(x, layer, norm_mix[layer], od_w_in[j], od_w_out[j], ret_head_norm[j], df_lambda[j],
                           df_q_norm[j], df_k_norm[j], df_subln[j])
        mk, mv = norm_matmul(mem.reshape(B * M, D), mem_norm, xa_w_kv[layer].astype(BF16), [W, W], [BF16, BF16])
        x = memory_xattn(x, mk.reshape(B, M, W), mv.reshape(B, M, W), norm_mem[layer], xa_w_q[layer].astype(BF16),
                         xa_q_norm[layer], xa_k_norm[layer], xa_w_o[layer].astype(BF16))
        x = hier_moe_residual(x.reshape(B * S, D), norm_ffn[layer], moe_w_group[layer], moe_w_expert[layer],
                              moe_w1[layer], moe_w3[layer], moe_w2[layer]).reshape(B, S, D)
    return x


def kernel(x_prompt, x_sample, mem_prompt, mem_sample, mem_norm, norm_mix, norm_mem, norm_ffn, ev_w_in, ev_gate_bias, ev_w_out, ml_head_norm, mla_qa_norm, mla_kva_norm, mla_w_uq, mla_w_ukv, mla_q_norm, mla_k_norm, od_w_in, od_w_out, ret_head_norm, df_lambda, df_q_norm, df_k_norm, df_subln, xa_w_q, xa_w_kv, xa_q_norm, xa_k_norm, xa_w_o, moe_w_group, moe_w_expert, moe_w1, moe_w3, moe_w2):
    weights = (mem_norm, norm_mix, norm_mem, norm_ffn, ev_w_in, ev_gate_bias, ev_w_out, ml_head_norm,
               mla_qa_norm, mla_kva_norm, mla_w_uq, mla_w_ukv, mla_q_norm, mla_k_norm, od_w_in, od_w_out,
               ret_head_norm, df_lambda, df_q_norm, df_k_norm, df_subln, xa_w_q, xa_w_kv, xa_q_norm, xa_k_norm,
               xa_w_o, moe_w_group, moe_w_expert, moe_w1.astype(BF16), moe_w3.astype(BF16), moe_w2.astype(BF16))
    return (_trunk(x_prompt, mem_prompt, *weights), _trunk(x_sample, mem_sample, *weights))
```
